```python
import jax, jax.numpy as jnp
from jax import lax
import numpy as np

D_MODEL = 1024
BATCH = 2
SEQ = 8192
DEPTH = 4
DEC_BATCH = 128
DEC_SEQ = 1
PAST_LEN = 2048
PAGE_SIZE = 128

H_A = 8
DH_A = 64
D_A = H_A * DH_A
MOBA_BLOCK = 256
MOBA_TOPK = 3
Q_BLOCK = 128
H_M = 4
DH_M = 128
D_M = H_M * DH_M
MLSTM_CHUNK = 128
D_C = 512
CONV_W = 31
N_BRANCH = 3
EPS = 1e-6
NEG = -1e30
IN_SPLITS = (D_A, D_A, D_A, D_A, D_M, D_M, D_M, D_M, D_M, H_M, H_M, D_C, D_C, D_C, N_BRANCH * D_MODEL)
N_IN = 4 * D_A + 5 * D_M + 2 * H_M + 3 * D_C + N_BRANCH * D_MODEL

kernel_name = 'hybrid_moba_mlstm_conformer_step'


def rmsnorm(x, g):
    xf = x.astype(jnp.float32)
    y = xf * lax.rsqrt(jnp.mean(xf * xf, axis=-1, keepdims=True) + EPS)
    return (y * g.astype(jnp.float32)).astype(x.dtype)


def layernorm_f32(x, g, b):
    xf = x.astype(jnp.float32)
    mu = jnp.mean(xf, axis=-1, keepdims=True)
    var = jnp.mean(jnp.square(xf - mu), axis=-1, keepdims=True)
    return (xf - mu) * lax.rsqrt(var + EPS) * g.astype(jnp.float32) + b.astype(jnp.float32)


def heads(t, n):
    B, T, W = t.shape
    return t.reshape(B, T, n, W // n).transpose(0, 2, 1, 3)


def moba_attention(q, k, v, q_pos):
    B, H, T, dh = k.shape
    nb = -(-T // MOBA_BLOCK)
    pad = nb * MOBA_BLOCK - T
    kb = jnp.pad(k, ((0, 0), (0, 0), (0, pad), (0, 0))).reshape(B, H, nb, MOBA_BLOCK, dh)
    vb = jnp.pad(v, ((0, 0), (0, 0), (0, pad), (0, 0))).reshape(B, H, nb, MOBA_BLOCK, dh)
    kmean = jnp.mean(kb.astype(jnp.float32), axis=3)
    k_eff = min(MOBA_TOPK, nb)
    gather = jax.vmap(jax.vmap(lambda blocks, idx: blocks[idx]))
    scale = DH_A ** -0.5

    def attend_block(args):
        qb, pos = args
        QB = qb.shape[2]
        own = pos // MOBA_BLOCK
        gs = jnp.einsum('bhqd,bhnd->bhqn', qb.astype(jnp.float32), kmean)
        past = jnp.arange(nb)[None, :] < own[:, None]
        gs = jnp.where(past, gs, -jnp.inf)
        _, idx = lax.top_k(gs, k_eff)
        valid = jnp.arange(k_eff)[None, :] < own[:, None]
        k_sel = gather(kb, idx)
        v_sel = gather(vb, idx)
        k_own = kb[:, :, own]
        v_own = vb[:, :, own]
        s_sel = jnp.einsum('bhqd,bhqjkd->bhqjk', qb, k_sel).reshape(B, H, QB, k_eff * MOBA_BLOCK)
        s_own = jnp.einsum('bhqd,bhqkd->bhqk', qb, k_own)
        m_sel = jnp.repeat(valid, MOBA_BLOCK, axis=1)
        m_own = (own[:, None] * MOBA_BLOCK + jnp.arange(MOBA_BLOCK)[None, :]) <= pos[:, None]
        s = jnp.concatenate([s_sel, s_own], axis=-1).astype(jnp.float32) * scale
        mask = jnp.concatenate([m_sel, m_own], axis=-1)
        p = jax.nn.softmax(jnp.where(mask, s, -jnp.inf), axis=-1).astype(v.dtype)
        p_sel = p[..., :k_eff * MOBA_BLOCK].reshape(B, H, QB, k_eff, MOBA_BLOCK)
        p_own = p[..., k_eff * MOBA_BLOCK:]
        return (jnp.einsum('bhqjk,bhqjkd->bhqd', p_sel, v_sel)
                + jnp.einsum('bhqk,bhqkd->bhqd', p_own, v_own))

    Q = q.shape[2]
    qblk = min(Q_BLOCK, Q)
    nq = -(-Q // qblk)
    qpad = nq * qblk - Q
    qs = jnp.pad(q, ((0, 0), (0, 0), (0, qpad), (0, 0))).reshape(B, H, nq, qblk, dh).transpose(2, 0, 1, 3, 4)
    ps = jnp.pad(q_pos, (0, qpad), mode='edge').reshape(nq, qblk)
    out = lax.map(attend_block, (qs, ps))
    return out.transpose(1, 2, 0, 3, 4).reshape(B, H, nq * qblk, dh)[:, :, :Q]


def mlstm_chunkwise(q, k, v, ig, lf, C0, n0, m0):
    B, H, T, dh = q.shape
    L = min(MLSTM_CHUNK, T)
    nc = -(-T // L)
    pad = nc * L - T
    p4 = ((0, 0), (0, 0), (0, pad), (0, 0))
    p3 = ((0, 0), (0, 0), (0, pad))
    qc = jnp.moveaxis(jnp.pad(q, p4).reshape(B, H, nc, L, dh), 2, 0)
    kc = jnp.moveaxis(jnp.pad(k, p4).reshape(B, H, nc, L, dh), 2, 0)
    vc = jnp.moveaxis(jnp.pad(v, p4).reshape(B, H, nc, L, dh), 2, 0)
    ic = jnp.moveaxis(jnp.pad(ig, p3, constant_values=NEG).reshape(B, H, nc, L), 2, 0)
    fc = jnp.moveaxis(jnp.pad(lf, p3).reshape(B, H, nc, L), 2, 0)
    causal = jnp.tril(jnp.ones((L, L), dtype=bool))

    def step(carry, xs):
        C, n, m = carry
        qx, kx, vx, ix, fx = xs
        b = jnp.cumsum(fx, axis=-1)
        dmat = jnp.where(causal, b[..., :, None] - b[..., None, :] + ix[..., None, :], NEG)
        inter = b + m[..., None]
        mt = jnp.maximum(inter, jnp.max(dmat, axis=-1))
        w = jnp.exp(dmat - mt[..., None])
        dec = jnp.exp(inter - mt)
        s = jnp.einsum('bhtd,bhsd->bhts', qx, kx) * w
        num = jnp.einsum('bhts,bhsd->bhtd', s, vx) + dec[..., None] * jnp.einsum('bhtd,bhde->bhte', qx, C)
        den = jnp.sum(s, axis=-1) + dec * jnp.einsum('bhtd,bhd->bht', qx, n)
        h = num / jnp.maximum(jnp.abs(den), jnp.exp(-mt))[..., None]
        m_end = mt[..., -1]
        dec_end = jnp.exp(b[..., -1] + m - m_end)
        ws = jnp.exp(b[..., -1:] - b + ix - m_end[..., None])
        C_new = dec_end[..., None, None] * C + jnp.einsum('bhs,bhsd,bhse->bhde', ws, kx, vx)
        n_new = dec_end[..., None] * n + jnp.einsum('bhs,bhsd->bhd', ws, kx)
        return (C_new, n_new, m_end), h

    (C, n, m), hs = lax.scan(step, (C0, n0, m0), (qc, kc, vc, ic, fc))
    h = jnp.moveaxis(hs, 0, 2).reshape(B, H, nc * L, dh)[:, :, :T]
    return h, C, n, m


def causal_dwconv(u, buf, w, b):
    full = jnp.concatenate([buf.astype(u.dtype), u], axis=1)
    y = lax.conv_general_dilated(full, w.astype(u.dtype)[:, None, :], window_strides=(1,), padding='VALID',
                                 dimension_numbers=('NWC', 'WIO', 'NWC'), feature_group_count=u.shape[-1])
    return y + b.astype(u.dtype), full[:, -(CONV_W - 1):]


def hybrid_layer(x, p, k_past, v_past, mstate, conv_buf, pos0):
    B, T, _ = x.shape
    f32 = jnp.float32
    h = rmsnorm(x, p['norm_g'])
    proj = h @ p['w_in']
    offsets = np.cumsum(IN_SPLITS)[:-1].tolist()
    (qa, ka, va, za, qm, km, vm, om, zm, im, fm, ga, gb, zc, gates) = jnp.split(proj, offsets, axis=-1)

    qa = rmsnorm(heads(qa, H_A), p['qn_g'])
    ka = rmsnorm(heads(ka, H_A), p['kn_g'])
    va = heads(va, H_A)
    if k_past is None:
        k_all, v_all = ka, va
    else:
        k_all = jnp.concatenate([k_past.astype(ka.dtype), ka], axis=2)
        v_all = jnp.concatenate([v_past.astype(va.dtype), va], axis=2)
    q_pos = pos0 + jnp.arange(T)
    ao = moba_attention(qa, k_all, v_all, q_pos).transpose(0, 2, 1, 3).reshape(B, T, D_A)
    y_a = (ao * jax.nn.silu(za)) @ p['w_a_out']

    qm_h = heads(qm, H_M).astype(f32)
    km_h = heads(km, H_M).astype(f32) * (DH_M ** -0.5)
    vm_h = heads(vm, H_M).astype(f32)
    ig = (im + p['b_ig']).astype(f32).transpose(0, 2, 1)
    lf = jax.nn.log_sigmoid((fm + p['b_fg']).astype(f32)).transpose(0, 2, 1)
    C0, n0, m0 = mstate
    hm, C1, n1, m1 = mlstm_chunkwise(qm_h, km_h, vm_h, ig, lf, C0.astype(f32), n0.astype(f32), m0.astype(f32))
    mu = jnp.mean(hm, axis=-1, keepdims=True)
    var = jnp.mean(jnp.square(hm - mu), axis=-1, keepdims=True)
    hm = ((hm - mu) * lax.rsqrt(var + EPS)).transpose(0, 2, 1, 3).reshape(B, T, D_M) * p['mh_norm_g'].astype(f32)
    hm = hm * jax.nn.sigmoid(om.astype(f32)) * jax.nn.silu(zm.astype(f32))
    y_m = hm.astype(x.dtype) @ p['w_m_out']

    u = ga * jax.nn.sigmoid(gb)
    c, new_buf = causal_dwconv(u, conv_buf, p['conv_w'], p['conv_b'])
    c = jax.nn.silu(layernorm_f32(c, p['cln_g'], p['cln_b'])) * jax.nn.silu(zc.astype(f32))
    y_c = c.astype(x.dtype) @ p['w_c_out']

    g = jax.nn.sigmoid(gates.astype(f32)).reshape(B, T, N_BRANCH, D_MODEL)
    merged = (g[:, :, 0] * y_a + g[:, :, 1] * y_m + g[:, :, 2] * y_c).astype(x.dtype)
    out = x + merged @ p['w_out']
    return out, ka, va, C1, n1, m1, new_buf


def setup_inputs(seed: int = 0) -> dict:
    key = jax.random.key(seed)
    ks = jax.random.split(key, 26)
    f32 = jnp.float32
    n_pages = PAST_LEN // PAGE_SIZE
    n_used = DEC_BATCH * n_pages
    n_pool = n_used + n_used // 4

    def nrm(k, shape, s):
        return s * jax.random.normal(k, shape, f32)

    return {
        'x_prompt': nrm(ks[0], (BATCH, SEQ, D_MODEL), 1.0),
        'x_sample': nrm(ks[1], (DEC_BATCH, DEC_SEQ, D_MODEL), 1.0),
        'cache_k': nrm(ks[2], (n_pool, DEPTH, PAGE_SIZE, H_A, DH_A), 1.0),
        'cache_v': nrm(ks[3], (n_pool, DEPTH, PAGE_SIZE, H_A, DH_A), 1.0),
        'state_mlstm_C': nrm(ks[4], (DEPTH, DEC_BATCH, H_M, DH_M, DH_M), 0.05),
        'state_mlstm_n': jnp.abs(nrm(ks[5], (DEPTH, DEC_BATCH, H_M, DH_M), 0.5)),
        'state_mlstm_m': nrm(ks[6], (DEPTH, DEC_BATCH, H_M), 0.5),
        'state_conv': nrm(ks[7], (DEPTH, DEC_BATCH, CONV_W - 1, D_C), 1.0),
        'page_table': jax.random.permutation(ks[8], n_pool)[:n_used].reshape(DEC_BATCH, n_pages).astype(jnp.int32),
        'norm_g': 1.0 + nrm(ks[9], (DEPTH, D_MODEL), 0.02),
        'w_in': nrm(ks[10], (DEPTH, D_MODEL, N_IN), D_MODEL ** -0.5),
        'b_igate': nrm(ks[11], (DEPTH, H_M), 0.1),
        'b_fgate': 3.0 + nrm(ks[12], (DEPTH, H_M), 0.3),
        'qn_g': 1.0 + nrm(ks[13], (DEPTH, DH_A), 0.02),
        'kn_g': 1.0 + nrm(ks[14], (DEPTH, DH_A), 0.02),
        'w_a_out': nrm(ks[15], (DEPTH, D_A, D_MODEL), D_A ** -0.5),
        'mh_norm_g': 1.0 + nrm(ks[16], (DEPTH, D_M), 0.02),
        'w_m_out': nrm(ks[17], (DEPTH, D_M, D_MODEL), D_M ** -0.5),
        'conv_w': nrm(ks[18], (DEPTH, CONV_W, D_C), CONV_W ** -0.5),
        'conv_b': nrm(ks[19], (DEPTH, D_C), 0.02),
        'cln_g': 1.0 + nrm(ks[20], (DEPTH, D_C), 0.02),
        'cln_b': nrm(ks[21], (DEPTH, D_C), 0.02),
        'w_c_out': nrm(ks[22], (DEPTH, D_C, D_MODEL), D_C ** -0.5),
        'w_out': nrm(ks[23], (DEPTH, D_MODEL, D_MODEL), 0.5 * D_MODEL ** -0.5),
    }


def reference(x_prompt, x_sample, cache_k, cache_v, state_mlstm_C, state_mlstm_n, state_mlstm_m, state_conv,
              page_table, norm_g, w_in, b_igate, b_fgate, qn_g, kn_g, w_a_out, mh_norm_g, w_m_out,
              conv_w, conv_b, cln_g, cln_b, w_c_out, w_out):
    f32 = jnp.float32
    n_pages = page_table.shape[1]
    past_len = n_pages * PAGE_SIZE
    yp, ys = x_prompt, x_sample
    kp_l, vp_l, Cp_l, np_l, mp_l, bp_l = [], [], [], [], [], []
    ks_l, vs_l, Cs_l, ns_l, ms_l, bs_l = [], [], [], [], [], []
    for l in range(DEPTH):
        p = {'norm_g': norm_g[l], 'w_in': w_in[l], 'b_ig': b_igate[l], 'b_fg': b_fgate[l],
             'qn_g': qn_g[l], 'kn_g': kn_g[l], 'w_a_out': w_a_out[l], 'mh_norm_g': mh_norm_g[l],
             'w_m_out': w_m_out[l], 'conv_w': conv_w[l], 'conv_b': conv_b[l], 'cln_g': cln_g[l],
             'cln_b': cln_b[l], 'w_c_out': w_c_out[l], 'w_out': w_out[l]}
        B = yp.shape[0]
        m_init = (jnp.zeros((B, H_M, DH_M, DH_M), f32), jnp.zeros((B, H_M, DH_M), f32), jnp.zeros((B, H_M), f32))
        buf_init = jnp.zeros((B, CONV_W - 1, D_C), yp.dtype)
        yp, ka, va, C1, n1, m1, nb1 = hybrid_layer(yp, p, None, None, m_init, buf_init, 0)
        kp_l.append(ka.transpose(0, 2, 1, 3)); vp_l.append(va.transpose(0, 2, 1, 3))
        Cp_l.append(C1.astype(state_mlstm_C.dtype)); np_l.append(n1.astype(state_mlstm_n.dtype))
        mp_l.append(m1.astype(state_mlstm_m.dtype)); bp_l.append(nb1.astype(state_conv.dtype))
        DB = ys.shape[0]
        k_past = cache_k[page_table, l].reshape(DB, past_len, H_A, DH_A).transpose(0, 2, 1, 3)
        v_past = cache_v[page_table, l].reshape(DB, past_len, H_A, DH_A).transpose(0, 2, 1, 3)
        ms = (state_mlstm_C[l], state_mlstm_n[l], state_mlstm_m[l])
        ys, ka, va, C1, n1, m1, nb1 = hybrid_layer(ys, p, k_past, v_past, ms, state_conv[l], past_len)
        ks_l.append(ka.transpose(0, 2, 1, 3)); vs_l.append(va.transpose(0, 2, 1, 3))
        Cs_l.append(C1.astype(state_mlstm_C.dtype)); ns_l.append(n1.astype(state_mlstm_n.dtype))
        ms_l.append(m1.astype(state_mlstm_m.dtype)); bs_l.append(nb1.astype(state_conv.dtype))
    k_prompt = jnp.stack(kp_l, axis=1)
    v_prompt = jnp.stack(vp_l, axis=1)
    C_prompt = jnp.stack(Cp_l, axis=0)
    n_prompt = jnp.stack(np_l, axis=0)
    m_prompt = jnp.stack(mp_l, axis=0)
    conv_prompt = jnp.stack(bp_l, axis=0)
    k_sample = jnp.stack(ks_l, axis=1)
    v_sample = jnp.stack(vs_l, axis=1)
    C_sample = jnp.stack(Cs_l, axis=0)
    n_sample = jnp.stack(ns_l, axis=0)
    m_sample = jnp.stack(ms_l, axis=0)
    conv_sample = jnp.stack(bs_l, axis=0)
    return (yp, ys, k_prompt, v_prompt, C_prompt, n_prompt, m_prompt, conv_prompt,
            k_sample, v_sample, C_sample, n_sample, m_sample, conv_sample)
```

```python
import functools

import numpy as np
import jax
import jax.numpy as jnp
from jax import lax
from jax.experimental import pallas as pl
from jax.experimental.pallas import tpu as pltpu

F32 = jnp.float32
BF16 = jnp.bfloat16

D_MODEL = 1024
H_A = 8
DH_A = 64
D_A = H_A * DH_A
MOBA_BLOCK = 256
MOBA_TOPK = 3
Q_BLOCK = 128
H_M = 4
DH_M = 128
D_M = H_M * DH_M
MLSTM_CHUNK = 128
D_C = 512
CONV_W = 31
N_BRANCH = 3
PAGE_SIZE = 128
EPS = 1e-6
NEG = -1e30

SEG = 512
N_SEG = 12 + N_BRANCH * D_MODEL // SEG
IF_COLS = 128
N_COLS = N_SEG * SEG + IF_COLS
HALO = 32
LANES = 128

_NT = (((1,), (1,)), ((), ()))

_VMEM_LIMIT = 52 * 1024 * 1024


def _cp(sem, vmem=_VMEM_LIMIT):
    return pltpu.CompilerParams(dimension_semantics=sem, vmem_limit_bytes=vmem)


def _sigmoid(x):
    return 1.0 / (1.0 + jnp.exp(-x))


def _silu(x):
    return x * _sigmoid(x)


def _log_sigmoid(x):
    return jnp.minimum(x, 0.0) - jnp.log(1.0 + jnp.exp(-jnp.abs(x)))


def _dot(a, b):
    return jnp.dot(a, b, preferred_element_type=F32)


def _dot_nt(a, b):
    return lax.dot_general(a, b, _NT, preferred_element_type=F32)


def _split2(x):
    hi = x.astype(BF16)
    lo = (x - hi.astype(F32)).astype(BF16)
    return hi, lo


def _split3(x):
    hi = x.astype(BF16)
    r = x - hi.astype(F32)
    mid = r.astype(BF16)
    lo = (r - mid.astype(F32)).astype(BF16)
    return hi, mid, lo


def _layernorm_rows(x):
    mu = jnp.mean(x, axis=-1, keepdims=True)
    xc = x - mu
    var = jnp.mean(xc * xc, axis=-1, keepdims=True)
    return xc * lax.rsqrt(var + EPS)


def _proj_kernel(x_ref, ng_ref, w_ref, gbd_ref, qg_ref, kg_ref, ifb_ref,
                 q_ref, kf_ref, kb_ref, vf_ref, vb_ref, sza_ref, qm_ref, km_ref, vm_ref,
                 gm_ref, if_ref, u_ref, szc_ref, g_ref, *kmean_refs):
    x = x_ref[...]
    h = x * lax.rsqrt(jnp.mean(x * x, axis=-1, keepdims=True) + EPS) * ng_ref[...]
    hb = h.astype(BF16)

    def seg(i):
        return _dot(hb, w_ref[:, i * SEG:(i + 1) * SEG])

    def headnorm(t, g_row):
        hi, lo = _split2(t * t)
        ms = (_dot(hi, gbd_ref[...]) + _dot(lo, gbd_ref[...])) * (1.0 / DH_A)
        return t * lax.rsqrt(ms + EPS) * g_row

    qn = headnorm(seg(0), qg_ref[...])
    q_ref[...] = (qn * (DH_A ** -0.5)).astype(q_ref.dtype)
    kn = headnorm(seg(1), kg_ref[...])
    kf_ref[...] = kn
    kb_ref[...] = kn.astype(BF16)
    if kmean_refs:
        kmean_ref = kmean_refs[0]
        for r in range(kmean_ref.shape[0]):
            kmean_ref[r] = jnp.mean(kn[r * MOBA_BLOCK:(r + 1) * MOBA_BLOCK], axis=0, keepdims=True)
    v = seg(2)
    vf_ref[...] = v
    vb_ref[...] = v.astype(BF16)
    sza_ref[...] = _silu(seg(3)).astype(sza_ref.dtype)
    qm_ref[...] = seg(4).astype(qm_ref.dtype)
    km_ref[...] = (seg(5) * (DH_M ** -0.5)).astype(km_ref.dtype)
    vm_ref[...] = seg(6).astype(vm_ref.dtype)
    gm_ref[...] = (_sigmoid(seg(7)) * _silu(seg(8))).astype(gm_ref.dtype)
    u_ref[...] = seg(9) * _sigmoid(seg(10))
    szc_ref[...] = _silu(seg(11)).astype(szc_ref.dtype)
    for c in range(N_BRANCH * D_MODEL // SEG):
        g_ref[:, c * SEG:(c + 1) * SEG] = _sigmoid(seg(12 + c)).astype(g_ref.dtype)
    val = _dot(hb, w_ref[:, N_SEG * SEG:N_SEG * SEG + IF_COLS]) + ifb_ref[...]
    lane = lax.broadcasted_iota(jnp.int32, val.shape, 1)
    if_ref[...] = jnp.where(lane < H_M, val, _log_sigmoid(val))


def _proj_call(x2d, ng, w, gbd, qg, kg, ifb, *, tm, act_dtype, with_kmean):
    n_tok = x2d.shape[0]
    assert n_tok % tm == 0
    grid = (n_tok // tm,)
    row = lambda i: (i, 0)
    const = lambda i: (0, 0)
    resident = functools.partial(pl.BlockSpec, index_map=const, pipeline_mode=pl.Buffered(1))

    def tok(width, dtype):
        return pl.BlockSpec((tm, width), row), jax.ShapeDtypeStruct((n_tok, width), dtype)

    outs = [
        tok(D_A, act_dtype),
        tok(D_A, F32), tok(D_A, BF16),
        tok(D_A, F32), tok(D_A, BF16),
        tok(D_A, act_dtype),
        tok(D_M, act_dtype), tok(D_M, act_dtype), tok(D_M, act_dtype),
        tok(D_M, act_dtype),
        tok(IF_COLS, F32),
        tok(D_C, F32),
        tok(D_C, act_dtype),
        tok(N_BRANCH * D_MODEL, act_dtype),
    ]
    if with_kmean:
        assert tm % MOBA_BLOCK == 0
        nb = tm // MOBA_BLOCK
        outs.append((pl.BlockSpec((nb, 1, D_A), lambda i: (i, 0, 0)),
                     jax.ShapeDtypeStruct((n_tok // MOBA_BLOCK, 1, D_A), F32)))
    return pl.pallas_call(
        _proj_kernel,
        grid=grid,
        in_specs=[
            pl.BlockSpec((tm, D_MODEL), row),
            resident((1, D_MODEL)),
            resident((D_MODEL, N_COLS)),
            resident((D_A, D_A)),
            resident((1, D_A)),
            resident((1, D_A)),
            resident((1, IF_COLS)),
        ],
        out_specs=[o[0] for o in outs],
        out_shape=[o[1] for o in outs],
        compiler_params=_cp(("parallel",)),
        name="proj",
    )(x2d, ng, w, gbd, qg, kg, ifb)


def _out_kernel(a_ref, m_ref, c_ref, g_ref, x_ref, wa_ref, wm_ref, wc_ref, wo_ref, o_ref):
    ya = _dot(a_ref[...].astype(BF16), wa_ref[...])
    ym = _dot(m_ref[...].astype(BF16), wm_ref[...])
    yc = _dot(c_ref[...].astype(BF16), wc_ref[...])
    merged = (g_ref[:, 0:D_MODEL].astype(F32) * ya
              + g_ref[:, D_MODEL:2 * D_MODEL].astype(F32) * ym
              + g_ref[:, 2 * D_MODEL:3 * D_MODEL].astype(F32) * yc)
    o_ref[...] = x_ref[...] + _dot(merged.astype(BF16), wo_ref[...])


def _out_call(a, m, c, g, x2d, wa, wm, wc, wo, *, tm):
    n_tok = x2d.shape[0]
    row = lambda i: (i, 0)
    const = lambda i: (0, 0)
    resident = functools.partial(pl.BlockSpec, index_map=const, pipeline_mode=pl.Buffered(1))
    return pl.pallas_call(
        _out_kernel,
        grid=(n_tok // tm,),
        in_specs=[
            pl.BlockSpec((tm, D_A), row),
            pl.BlockSpec((tm, D_M), row),
            pl.BlockSpec((tm, D_C), row),
            pl.BlockSpec((tm, N_BRANCH * D_MODEL), row),
            pl.BlockSpec((tm, D_MODEL), row),
            resident((D_A, D_MODEL)),
            resident((D_M, D_MODEL)),
            resident((D_C, D_MODEL)),
            resident((D_MODEL, D_MODEL)),
        ],
        out_specs=pl.BlockSpec((tm, D_MODEL), row),
        out_shape=jax.ShapeDtypeStruct((n_tok, D_MODEL), F32),
        compiler_params=_cp(("parallel",)),
        name="outproj",
    )(a, m, c, g, x2d, wa, wm, wc, wo)


def _attn_kernel(q_ref, k_ref, v_ref, km_ref, sza_ref, o_ref, *, nb):
    i = pl.program_id(2)
    own = i // (MOBA_BLOCK // Q_BLOCK)
    half = i % (MOBA_BLOCK // Q_BLOCK)
    q2 = q_ref[...]
    lane = lax.broadcasted_iota(jnp.int32, (Q_BLOCK, LANES), 1)
    km_hi, km_mid, km_lo = _split3(km_ref[...])
    blk = lax.broadcasted_iota(jnp.int32, (Q_BLOCK, nb), 1).astype(F32)
    ownf = own.astype(F32)

    qs, biases = [], []
    for a in range(2):
        qa = jnp.where((lane // DH_A) == a, q2, jnp.zeros_like(q2))
        gs = _dot_nt(qa, km_hi) + _dot_nt(qa, km_mid) + _dot_nt(qa, km_lo)
        gs = jnp.where(blk < ownf, gs, -jnp.inf)
        bias = jnp.full((Q_BLOCK, nb), NEG, F32)
        for _ in range(MOBA_TOPK):
            mx = jnp.max(gs, axis=-1, keepdims=True)
            cand = jnp.where((gs == mx) & (mx > -jnp.inf), blk, float(nb))
            idx = jnp.min(cand, axis=-1, keepdims=True)
            pick = blk == idx
            bias = jnp.where(pick, 0.0, bias)
            gs = jnp.where(pick, -jnp.inf, gs)
        qs.append(qa)
        biases.append(bias)

    def kv(j):
        start = pl.multiple_of(j * MOBA_BLOCK, MOBA_BLOCK)
        return k_ref[pl.ds(start, MOBA_BLOCK), :], v_ref[pl.ds(start, MOBA_BLOCK), :]

    k_own, v_own = kv(own)
    rowi = lax.broadcasted_iota(jnp.int32, (Q_BLOCK, MOBA_BLOCK), 0) + half * Q_BLOCK
    coli = lax.broadcasted_iota(jnp.int32, (Q_BLOCK, MOBA_BLOCK), 1)
    causal = coli <= rowi
    state = []
    for a in range(2):
        s = jnp.where(causal, _dot_nt(qs[a], k_own), NEG)
        m = jnp.max(s, axis=-1, keepdims=True)
        p = jnp.exp(s - m)
        state += [m, jnp.sum(p, axis=-1, keepdims=True), _dot(p.astype(BF16), v_own)]

    def body(j, st):
        kj, vj = kv(j)
        jf = j.astype(F32)
        new = []
        for a in range(2):
            m, l, acc = st[3 * a:3 * a + 3]
            bcol = jnp.sum(jnp.where(blk == jf, biases[a], 0.0), axis=-1, keepdims=True)
            s = _dot_nt(qs[a], kj) + bcol
            m_new = jnp.maximum(m, jnp.max(s, axis=-1, keepdims=True))
            alpha = jnp.exp(m - m_new)
            p = jnp.exp(s - m_new)
            new += [m_new, alpha * l + jnp.sum(p, axis=-1, keepdims=True),
                    alpha * acc + _dot(p.astype(BF16), vj)]
        return tuple(new)

    st = lax.fori_loop(0, own, body, tuple(state))
    o = jnp.where((lane // DH_A) == 0, st[2] / st[1], st[5] / st[4])
    o_ref[...] = (o * sza_ref[...].astype(F32)).astype(o_ref.dtype)


def _attn_call(q, kb, vb, kmean, sza, *, batch, seq):
    assert seq % MOBA_BLOCK == 0
    nb = seq // MOBA_BLOCK
    nq = seq // Q_BLOCK
    n_pair = D_A // LANES
    return pl.pallas_call(
        functools.partial(_attn_kernel, nb=nb),
        grid=(batch, n_pair, nq),
        in_specs=[
            pl.BlockSpec((Q_BLOCK, LANES), lambda b, p, i: (b * nq + i, p)),
            pl.BlockSpec((seq, LANES), lambda b, p, i: (b, p)),
            pl.BlockSpec((seq, LANES), lambda b, p, i: (b, p)),
            pl.BlockSpec((None, nb, LANES), lambda b, p, i: (b, 0, p)),
            pl.BlockSpec((Q_BLOCK, LANES), lambda b, p, i: (b * nq + i, p)),
        ],
        out_specs=pl.BlockSpec((Q_BLOCK, LANES), lambda b, p, i: (b * nq + i, p)),
        out_shape=jax.ShapeDtypeStruct((batch * seq, D_A), BF16),
        compiler_params=_cp(("parallel", "parallel", "parallel")),
        name="moba_prompt",
    )(q, kb, vb, kmean.reshape(batch, nb, D_A), sza)


def _mlstm_kernel(q_ref, kt_ref, k_ref, v_ref, ig_ref, lf_ref, gm_ref, mg_ref,
                  h_ref, c_out, n_out, m_out, c_s, n_s, m_s):
    c = pl.program_id(2)
    L = MLSTM_CHUNK

    @pl.when(c == 0)
    def _():
        c_s[...] = jnp.zeros_like(c_s)
        n_s[...] = jnp.zeros_like(n_s)
        m_s[...] = jnp.zeros_like(m_s)

    q = q_ref[...]
    kt = kt_ref[...]
    k = k_ref[...]
    v = v_ref[...]
    ig = ig_ref[0]
    lf = lf_ref[0]
    t_i = lax.broadcasted_iota(jnp.int32, (L, L), 0)
    s_i = lax.broadcasted_iota(jnp.int32, (L, L), 1)
    causal = s_i <= t_i
    b_col = jnp.sum(jnp.where(causal, jnp.broadcast_to(lf, (L, L)), 0.0), axis=-1, keepdims=True)
    b_t = jnp.broadcast_to(b_col, (L, L))
    b_s = b_t.T
    m_prev = m_s[...]
    dmat = jnp.where(causal, b_t - b_s + jnp.broadcast_to(ig, (L, L)), NEG)
    inter = b_col + m_prev
    mt = jnp.maximum(inter, jnp.max(dmat, axis=-1, keepdims=True))
    w = jnp.exp(dmat - mt)
    dec = jnp.exp(inter - mt)
    s = _dot(q, kt) * w
    c_prev = c_s[...]
    num = _dot(s.astype(BF16), v) + dec * _dot(q, c_prev.astype(BF16))
    den = jnp.sum(s, axis=-1, keepdims=True) + dec * jnp.sum(q.astype(F32) * n_s[...], axis=-1, keepdims=True)
    hm = num / jnp.maximum(jnp.abs(den), jnp.exp(-mt))

    m_end = mt[L - 1:L, :]
    b_last = b_col[L - 1:L, :]
    b_row = b_s[0:1, :]
    dec_end = jnp.exp(b_last + m_prev - m_end)
    ws = jnp.exp(b_last - b_row + ig - m_end)
    kws = (kt.astype(F32) * ws).astype(BF16)
    c_s[...] = dec_end * c_prev + _dot(kws, v)
    ws_hi, ws_lo = _split2(jnp.broadcast_to(ws, (8, L)))
    n_s[...] = dec_end * n_s[...] + (_dot(ws_hi, k) + _dot(ws_lo, k))[0:1, :]
    m_s[...] = m_end

    hn = _layernorm_rows(hm) * mg_ref[...] * gm_ref[...].astype(F32)
    h_ref[...] = hn.astype(h_ref.dtype)

    @pl.when(c == pl.num_programs(2) - 1)
    def _():
        c_out[...] = c_s[...]
        n_out[...] = n_s[...]
        m_out[...] = jnp.broadcast_to(m_s[...], m_out.shape)


def _mlstm_call(qm, km, vm, igf, gm, mg, *, batch, seq):
    L = MLSTM_CHUNK
    assert seq % L == 0
    nc = seq // L
    kt = km.reshape(batch, seq, D_M).transpose(0, 2, 1)
    gates = igf[:, :2 * H_M].reshape(batch, nc, L, 2 * H_M).transpose(0, 3, 1, 2)
    ig = gates[:, :H_M].reshape(batch * H_M * nc, 1, L)
    lf = gates[:, H_M:].reshape(batch * H_M * nc, 1, L)
    tile = pl.BlockSpec((L, DH_M), lambda b, h, c: (b * nc + c, h))
    gate = pl.BlockSpec((1, 1, L), lambda b, h, c: ((b * H_M + h) * nc + c, 0, 0))
    state = lambda r: pl.BlockSpec((None, None, r, DH_M), lambda b, h, c: (b, h, 0, 0))
    return pl.pallas_call(
        _mlstm_kernel,
        grid=(batch, H_M, nc),
        in_specs=[
            tile,
            pl.BlockSpec((None, DH_M, L), lambda b, h, c: (b, h, c)),
            tile, tile, gate, gate, tile,
            pl.BlockSpec((1, DH_M), lambda b, h, c: (0, h)),
        ],
        out_specs=[tile, state(DH_M), state(1), state(1)],
        out_shape=[
            jax.ShapeDtypeStruct((batch * seq, D_M), BF16),
            jax.ShapeDtypeStruct((batch, H_M, DH_M, DH_M), F32),
            jax.ShapeDtypeStruct((batch, H_M, 1, DH_M), F32),
            jax.ShapeDtypeStruct((batch, H_M, 1, DH_M), F32),
        ],
        scratch_shapes=[pltpu.VMEM((DH_M, DH_M), F32), pltpu.VMEM((1, DH_M), F32), pltpu.VMEM((1, 1), F32)],
        compiler_params=_cp(("parallel", "parallel", "arbitrary")),
        name="mlstm_prompt",
    )(qm, kt, km, vm, ig, lf, gm, mg)


def _conv_epilogue(c, lg_ref, lb_ref, szc_ref, o_ref):
    y = _layernorm_rows(c) * lg_ref[...] + lb_ref[...]
    o_ref[...] = (_silu(y) * szc_ref[...].astype(F32)).astype(o_ref.dtype)


def _conv_kernel(u_ref, halo_ref, w_ref, cb_ref, lg_ref, lb_ref, szc_ref, o_ref, full):
    i = pl.program_id(1)
    tm = u_ref.shape[0]
    full[0:HALO, :] = jnp.where(i > 0, halo_ref[...], 0.0)
    full[HALO:HALO + tm, :] = u_ref[...]
    first = HALO - (CONV_W - 1)
    acc = jnp.zeros((tm, D_C), F32)
    for j in range(CONV_W):
        acc = acc + w_ref[j:j + 1, :] * full[first + j:first + j + tm, :]
    _conv_epilogue(acc + cb_ref[...], lg_ref, lb_ref, szc_ref, o_ref)


def _conv_call(u, wpad, cb, lg, lb, szc, *, batch, seq, tm):
    assert seq % tm == 0 and tm % HALO == 0
    nt = seq // tm
    row = lambda b, i: (b * nt + i, 0)
    const = lambda b, i: (0, 0)
    halo_idx = lambda b, i: (jnp.maximum((b * nt + i) * (tm // HALO) - 1, 0), 0)
    return pl.pallas_call(
        _conv_kernel,
        grid=(batch, nt),
        in_specs=[
            pl.BlockSpec((tm, D_C), row),
            pl.BlockSpec((HALO, D_C), halo_idx),
            pl.BlockSpec((HALO, D_C), const),
            pl.BlockSpec((1, D_C), const),
            pl.BlockSpec((1, D_C), const),
            pl.BlockSpec((1, D_C), const),
            pl.BlockSpec((tm, D_C), row),
        ],
        out_specs=pl.BlockSpec((tm, D_C), row),
        out_shape=jax.ShapeDtypeStruct((batch * seq, D_C), BF16),
        scratch_shapes=[pltpu.VMEM((HALO + tm, D_C), F32)],
        compiler_params=_cp(("parallel", "parallel")),
        name="conv_prompt",
    )(u, u, wpad, cb, lg, lb, szc)


def _attn_sample_kernel(pt_ref, q_ref, kn_ref, vn_ref, sza_ref, *refs, n_pages):
    del pt_ref
    k_refs = refs[:n_pages]
    v_refs = refs[n_pages:2 * n_pages]
    o_ref = refs[2 * n_pages]
    n_blk = n_pages * PAGE_SIZE // MOBA_BLOCK
    ppb = MOBA_BLOCK // PAGE_SIZE
    q = q_ref[0]
    lane = lax.broadcasted_iota(jnp.int32, (H_A, D_A), 1)
    sub = lax.broadcasted_iota(jnp.int32, (H_A, D_A), 0)
    hmask = (lane // DH_A) == sub
    qh = jnp.where(hmask, jnp.broadcast_to(q, (H_A, D_A)), 0.0)
    qhb = qh.astype(BF16)

    scores, colsum = [], []
    for pg in range(n_pages):
        kp = k_refs[pg][...]
        scores.append(_dot_nt(qhb, kp.astype(BF16)))
        colsum.append(jnp.sum(kp, axis=0, keepdims=True))

    gs = []
    for j in range(n_blk):
        tot = colsum[j * ppb]
        for r in range(1, ppb):
            tot = tot + colsum[j * ppb + r]
        gs.append(jnp.sum(qh * (tot * (1.0 / MOBA_BLOCK)), axis=-1, keepdims=True))

    picked = [jnp.zeros((H_A, 1), F32) for _ in range(n_blk)]
    for _ in range(min(MOBA_TOPK, n_blk)):
        mx = gs[0]
        for j in range(1, n_blk):
            mx = jnp.maximum(mx, gs[j])
        found = jnp.zeros((H_A, 1), F32)
        for j in range(n_blk):
            hit = jnp.where((gs[j] == mx) & (found == 0.0), 1.0, 0.0)
            picked[j] = jnp.maximum(picked[j], hit)
            found = jnp.maximum(found, hit)
            gs[j] = jnp.where(hit > 0.0, -jnp.inf, gs[j])

    s_own = jnp.sum(qh * kn_ref[0], axis=-1, keepdims=True)
    m = s_own
    for pg in range(n_pages):
        scores[pg] = scores[pg] + jnp.where(picked[pg // ppb] > 0.0, 0.0, NEG)
        m = jnp.maximum(m, jnp.max(scores[pg], axis=-1, keepdims=True))
    p_own = jnp.exp(s_own - m)
    l = p_own
    o = p_own * vn_ref[0]
    for pg in range(n_pages):
        p = jnp.exp(scores[pg] - m)
        l = l + jnp.sum(p, axis=-1, keepdims=True)
        o = o + _dot(p.astype(BF16), v_refs[pg][...].astype(BF16))
    o = o / l
    orow = jnp.sum(jnp.where(hmask, o, 0.0), axis=0, keepdims=True)
    o_ref[0] = orow * sza_ref[0]


def _attn_sample_call(page_table, q, kn, vn, sza, cache_k, cache_v, layer):
    db, n_pages = page_table.shape
    assert MOBA_BLOCK % PAGE_SIZE == 0 and (n_pages * PAGE_SIZE) % MOBA_BLOCK == 0
    n_pool, depth = cache_k.shape[:2]
    ck = cache_k.reshape(n_pool, depth, PAGE_SIZE, D_A)
    cv = cache_v.reshape(n_pool, depth, PAGE_SIZE, D_A)
    rowspec = pl.BlockSpec((1, 1, D_A), lambda b, pt: (b, 0, 0))

    def page(pg):
        return pl.BlockSpec((None, None, PAGE_SIZE, D_A), lambda b, pt: (pt[b, pg], layer, 0, 0))

    grid_spec = pltpu.PrefetchScalarGridSpec(
        num_scalar_prefetch=1,
        grid=(db,),
        in_specs=[rowspec] * 4 + [page(pg) for pg in range(n_pages)] * 2,
        out_specs=rowspec,
    )
    r3 = lambda t: t.reshape(db, 1, D_A)
    return pl.pallas_call(
        functools.partial(_attn_sample_kernel, n_pages=n_pages),
        grid_spec=grid_spec,
        out_shape=jax.ShapeDtypeStruct((db, 1, D_A), F32),
        compiler_params=_cp(("parallel",)),
        name="moba_sample",
    )(page_table, r3(q), r3(kn), r3(vn), r3(sza), *([ck] * n_pages), *([cv] * n_pages)).reshape(db, D_A)


def _mlstm_sample_kernel(q_ref, k_ref, v_ref, if_ref, gm_ref, mg_ref, c_ref, n_ref, m_ref,
                         h_ref, c_out, n_out, m_out):
    bb = q_ref.shape[0]

    def body(bi, carry):
        for h in range(H_M):
            sl = slice(h * DH_M, (h + 1) * DH_M)
            q = q_ref[bi, :, sl]
            k = k_ref[bi, :, sl]
            v = v_ref[bi, :, sl]
            ig = if_ref[bi, :, h:h + 1]
            lf = if_ref[bi, :, H_M + h:H_M + h + 1]
            m0 = m_ref[bi, :, h:h + 1]
            c0 = c_ref[bi, h]
            n0 = n_ref[bi, h:h + 1, :]
            q_cols = jnp.broadcast_to(q, (DH_M, DH_M)).T
            k_cols = jnp.broadcast_to(k, (DH_M, DH_M)).T
            inter = lf + m0
            mt = jnp.maximum(inter, ig)
            w = jnp.exp(ig - mt)
            dec = jnp.exp(inter - mt)
            s = jnp.sum(q * k, axis=-1, keepdims=True) * w
            qc = jnp.sum(q_cols * c0, axis=0, keepdims=True)
            num = s * v + dec * qc
            den = s + dec * jnp.sum(q * n0, axis=-1, keepdims=True)
            hm = num / jnp.maximum(jnp.abs(den), jnp.exp(-mt))
            c_out[bi, h] = dec * c0 + w * (k_cols * v)
            n_out[bi, h:h + 1, :] = dec * n0 + w * k
            m_out[bi, :, h:h + 1] = mt
            h_ref[bi, :, sl] = _layernorm_rows(hm) * mg_ref[:, sl] * gm_ref[bi, :, sl]
        return carry

    lax.fori_loop(0, bb, body, 0)


def _mlstm_sample_call(qm, km, vm, igf, gm, mg, c0, n0, m0, layer, *, bb):
    db = qm.shape[0]
    assert db % bb == 0
    depth = m0.shape[0]
    row3 = lambda i: (i, 0, 0)
    r3 = lambda t: t.reshape(db, 1, t.shape[-1])
    tokspec = pl.BlockSpec((bb, 1, D_M), row3)
    h3, c1, n1, m1 = pl.pallas_call(
        _mlstm_sample_kernel,
        grid=(db // bb,),
        in_specs=[
            tokspec, tokspec, tokspec,
            pl.BlockSpec((bb, 1, IF_COLS), row3),
            tokspec,
            pl.BlockSpec((1, D_M), lambda i: (0, 0)),
            pl.BlockSpec((None, bb, H_M, DH_M, DH_M), lambda i: (layer, i, 0, 0, 0)),
            pl.BlockSpec((None, bb, H_M, DH_M), lambda i: (layer, i, 0, 0)),
            pl.BlockSpec((None, bb, 1, H_M), lambda i: (layer, i, 0, 0)),
        ],
        out_specs=[
            tokspec,
            pl.BlockSpec((bb, H_M, DH_M, DH_M), lambda i: (i, 0, 0, 0)),
            pl.BlockSpec((bb, H_M, DH_M), row3),
            pl.BlockSpec((bb, 1, H_M), row3),
        ],
        out_shape=[
            jax.ShapeDtypeStruct((db, 1, D_M), F32),
            jax.ShapeDtypeStruct((db, H_M, DH_M, DH_M), F32),
            jax.ShapeDtypeStruct((db, H_M, DH_M), F32),
            jax.ShapeDtypeStruct((db, 1, H_M), F32),
        ],
        compiler_params=_cp(("parallel",)),
        name="mlstm_sample",
    )(r3(qm), r3(km), r3(vm), r3(igf), r3(gm), mg, c0, n0, m0.reshape(depth, db, 1, H_M))
    return h3.reshape(db, D_M), c1, n1, m1.reshape(db, H_M)


def _conv_sample_kernel(st_ref, u_ref, w_ref, cb_ref, lg_ref, lb_ref, szc_ref, o_ref):
    acc = w_ref[CONV_W - 1:CONV_W, :] * u_ref[...]
    for j in range(CONV_W - 1):
        acc = acc + w_ref[j:j + 1, :] * st_ref[j]
    _conv_epilogue(acc + cb_ref[...], lg_ref, lb_ref, szc_ref, o_ref)


def _conv_sample_call(state_t, u, wpad, cb, lg, lb, szc, *, bb):
    db = u.shape[0]
    row = lambda i: (i, 0)
    const = lambda i: (0, 0)
    return pl.pallas_call(
        _conv_sample_kernel,
        grid=(db // bb,),
        in_specs=[
            pl.BlockSpec((CONV_W - 1, bb, D_C), lambda i: (0, i, 0)),
            pl.BlockSpec((bb, D_C), row),
            pl.BlockSpec((HALO, D_C), const),
            pl.BlockSpec((1, D_C), const),
            pl.BlockSpec((1, D_C), const),
            pl.BlockSpec((1, D_C), const),
            pl.BlockSpec((bb, D_C), row),
        ],
        out_specs=pl.BlockSpec((bb, D_C), row),
        out_shape=jax.ShapeDtypeStruct((db, D_C), F32),
        compiler_params=_cp(("parallel",)),
        name="conv_sample",
    )(state_t, u, wpad, cb, lg, lb, szc)


def _prep_weights(norm_g, w_in, b_igate, b_fgate, qn_g, kn_g, w_a_out, mh_norm_g, w_m_out,
                  conv_w, conv_b, cln_g, cln_b, w_c_out, w_out):
    depth = w_in.shape[0]
    gate0 = 4 * D_A + 5 * D_M
    w = jnp.concatenate([
        w_in[:, :, :gate0], w_in[:, :, gate0 + 2 * H_M:], w_in[:, :, gate0:gate0 + 2 * H_M],
        jnp.zeros((depth, D_MODEL, IF_COLS - 2 * H_M), w_in.dtype)], axis=-1).astype(BF16)
    assert w.shape[-1] == N_COLS
    ifb = jnp.concatenate([b_igate, b_fgate, jnp.zeros((depth, IF_COLS - 2 * H_M), F32)], axis=-1)
    gbd = jnp.asarray(np.kron(np.eye(H_A), np.ones((DH_A, DH_A))), BF16)
    wpad = jnp.concatenate([conv_w, jnp.zeros((depth, HALO - CONV_W, D_C), conv_w.dtype)], axis=1)
    layers = []
    for l in range(depth):
        layers.append(dict(
            ng=norm_g[l][None], w=w[l], gbd=gbd, ifb=ifb[l][None],
            qg=jnp.tile(qn_g[l], H_A)[None], kg=jnp.tile(kn_g[l], H_A)[None],
            wa=w_a_out[l].astype(BF16), wm=w_m_out[l].astype(BF16), wc=w_c_out[l].astype(BF16),
            wo=w_out[l].astype(BF16), mg=mh_norm_g[l][None], wpad=wpad[l],
            cb=conv_b[l][None], lg=cln_g[l][None], lb=cln_b[l][None]))
    return layers


def _prompt_layer(x2d, p, *, batch, seq):
    (q, kf, kb, vf, vb, sza, qm, km, vm, gm, igf, u, szc, g, kmean) = _proj_call(
        x2d, p["ng"], p["w"], p["gbd"], p["qg"], p["kg"], p["ifb"],
        tm=MOBA_BLOCK, act_dtype=BF16, with_kmean=True)
    a_in = _attn_call(q, kb, vb, kmean, sza, batch=batch, seq=seq)
    m_in, c1, n1, m1 = _mlstm_call(qm, km, vm, igf, gm, p["mg"], batch=batch, seq=seq)
    c_in = _conv_call(u, p["wpad"], p["cb"], p["lg"], p["lb"], szc, batch=batch, seq=seq, tm=MOBA_BLOCK)
    y = _out_call(a_in, m_in, c_in, g, x2d, p["wa"], p["wm"], p["wc"], p["wo"], tm=MOBA_BLOCK)
    conv_state = u.reshape(batch, seq, D_C)[:, seq - (CONV_W - 1):]
    return y, kf, vf, c1, n1[:, :, 0], m1[:, :, 0, 0], conv_state


def _sample_layer(x2d, p, page_table, cache_k, cache_v, st_c, st_n, st_m, st_conv, layer):
    db = x2d.shape[0]
    (q, kf, _, vf, _, sza, qm, km, vm, gm, igf, u, szc, g) = _proj_call(
        x2d, p["ng"], p["w"], p["gbd"], p["qg"], p["kg"], p["ifb"],
        tm=db, act_dtype=F32, with_kmean=False)
    a_in = _attn_sample_call(page_table, q, kf, vf, sza, cache_k, cache_v, layer)
    m_in, c1, n1, m1 = _mlstm_sample_call(qm, km, vm, igf, gm, p["mg"], st_c, st_n, st_m, layer, bb=8)
    conv_hist = st_conv[layer]
    c_in = _conv_sample_call(conv_hist.transpose(1, 0, 2), u, p["wpad"], p["cb"], p["lg"], p["lb"], szc, bb=8)
    y = _out_call(a_in, m_in, c_in, g, x2d, p["wa"], p["wm"], p["wc"], p["wo"], tm=db)
    conv_state = jnp.concatenate([conv_hist[:, 1:], u[:, None, :]], axis=1)
    return y, kf, vf, c1, n1, m1, conv_state


def kernel(x_prompt, x_sample, cache_k, cache_v, state_mlstm_C, state_mlstm_n, state_mlstm_m, state_conv,
           page_table, norm_g, w_in, b_igate, b_fgate, qn_g, kn_g, w_a_out, mh_norm_g, w_m_out,
           conv_w, conv_b, cln_g, cln_b, w_c_out, w_out):
    batch, seq, _ = x_prompt.shape
    db, dseq, _ = x_sample.shape
    assert dseq == 1
    depth = w_in.shape[0]
    layers = _prep_weights(norm_g, w_in, b_igate, b_fgate, qn_g, kn_g, w_a_out, mh_norm_g, w_m_out,
                           conv_w, conv_b, cln_g, cln_b, w_c_out, w_out)
    yp = x_prompt.reshape(batch * seq, D_MODEL)
    ys = x_sample.reshape(db, D_MODEL)
    pr, sa = [], []
    for l in range(depth):
        yp, *rest = _prompt_layer(yp, layers[l], batch=batch, seq=seq)
        pr.append(rest)
        ys, *rest = _sample_layer(ys, layers[l], page_table, cache_k, cache_v,
                                  state_mlstm_C, state_mlstm_n, state_mlstm_m, state_conv, l)
        sa.append(rest)

    def stack(rows, idx, axis):
        return jnp.stack([r[idx] for r in rows], axis=axis)

    k_prompt = jnp.stack([r[0].reshape(batch, seq, H_A, DH_A) for r in pr], axis=1)
    v_prompt = jnp.stack([r[1].reshape(batch, seq, H_A, DH_A) for r in pr], axis=1)
    k_sample = stack(sa, 0, 1).reshape(db, depth, 1, H_A, DH_A)
    v_sample = stack(sa, 1, 1).reshape(db, depth, 1, H_A, DH_A)
    return (yp.reshape(batch, seq, D_MODEL), ys.reshape(db, 1, D_MODEL),
            k_prompt, v_prompt, stack(pr, 2, 0), stack(pr, 3, 0), stack(pr, 4, 0), stack(pr, 5, 0),
            k_sample, v_sample, stack(sa, 2, 0), stack(sa, 3, 0), stack(sa, 4, 0), stack(sa, 5, 0))
```

```python
import functools

import numpy as np
import jax
import jax.numpy as jnp
from jax import lax
from jax.experimental import pallas as pl
from jax.experimental.pallas import tpu as pltpu

F32 = jnp.float32
BF16 = jnp.bfloat16

D_MODEL = 1024
H_A = 8
DH_A = 64
D_A = H_A * DH_A
MOBA_BLOCK = 256
MOBA_TOPK = 3
Q_BLOCK = 128
H_M = 4
DH_M = 128
D_M = H_M * DH_M
MLSTM_CHUNK = 128
D_C = 512
CONV_W = 31
N_BRANCH = 3
PAGE_SIZE = 128
EPS = 1e-6
NEG = -1e30

SEG = 512
N_SEG = 12 + N_BRANCH * D_MODEL // SEG
IF_COLS = 128
N_COLS = N_SEG * SEG + IF_COLS
HALO = 32
LANES = 128
ATTN_GROUP = 4

_NT = (((1,), (1,)), ((), ()))

_VMEM_LIMIT = 52 * 1024 * 1024


def _cp(sem, vmem=_VMEM_LIMIT):
    return pltpu.CompilerParams(dimension_semantics=sem, vmem_limit_bytes=vmem)


def _sigmoid(x):
    return 1.0 / (1.0 + jnp.exp(-x))


def _silu(x):
    return x * _sigmoid(x)


def _log_sigmoid(x):
    return jnp.minimum(x, 0.0) - jnp.log(1.0 + jnp.exp(-jnp.abs(x)))


def _dot(a, b):
    return jnp.dot(a, b, preferred_element_type=F32)


def _dot_nt(a, b):
    return lax.dot_general(a, b, _NT, preferred_element_type=F32)


def _split2(x):
    hi = x.astype(BF16)
    lo = (x - hi.astype(F32)).astype(BF16)
    return hi, lo


def _split3(x):
    hi = x.astype(BF16)
    r = x - hi.astype(F32)
    mid = r.astype(BF16)
    lo = (r - mid.astype(F32)).astype(BF16)
    return hi, mid, lo


def _layernorm_rows(x):
    mu = jnp.mean(x, axis=-1, keepdims=True)
    xc = x - mu
    var = jnp.mean(xc * xc, axis=-1, keepdims=True)
    return xc * lax.rsqrt(var + EPS)


def _proj_kernel(x_ref, ng_ref, w_ref, gbd_ref, qg_ref, kg_ref, ifb_ref,
                 q_ref, kf_ref, kb_ref, vf_ref, vb_ref, sza_ref, qm_ref, km_ref, vm_ref,
                 gm_ref, if_ref, u_ref, szc_ref, g_ref, *kmean_refs):
    x = x_ref[...]
    h = x * lax.rsqrt(jnp.mean(x * x, axis=-1, keepdims=True) + EPS) * ng_ref[...]
    hb = h.astype(BF16)

    def seg(i):
        return _dot(hb, w_ref[:, i * SEG:(i + 1) * SEG])

    def headnorm(t, g_row):
        hi, lo = _split2(t * t)
        ms = (_dot(hi, gbd_ref[...]) + _dot(lo, gbd_ref[...])) * (1.0 / DH_A)
        return t * lax.rsqrt(ms + EPS) * g_row

    qn = headnorm(seg(0), qg_ref[...])
    q_ref[...] = (qn * (DH_A ** -0.5)).astype(q_ref.dtype)
    kn = headnorm(seg(1), kg_ref[...])
    kf_ref[...] = kn
    kb_ref[...] = kn.astype(BF16)
    if kmean_refs:
        kmean_ref = kmean_refs[0]
        for r in range(kmean_ref.shape[0]):
            kmean_ref[r] = jnp.mean(kn[r * MOBA_BLOCK:(r + 1) * MOBA_BLOCK], axis=0, keepdims=True)
    v = seg(2)
    vf_ref[...] = v
    vb_ref[...] = v.astype(BF16)
    sza_ref[...] = _silu(seg(3)).astype(sza_ref.dtype)
    qm_ref[...] = seg(4).astype(qm_ref.dtype)
    km_ref[...] = (seg(5) * (DH_M ** -0.5)).astype(km_ref.dtype)
    vm_ref[...] = seg(6).astype(vm_ref.dtype)
    gm_ref[...] = (_sigmoid(seg(7)) * _silu(seg(8))).astype(gm_ref.dtype)
    u_ref[...] = seg(9) * _sigmoid(seg(10))
    szc_ref[...] = _silu(seg(11)).astype(szc_ref.dtype)
    for c in range(N_BRANCH * D_MODEL // SEG):
        g_ref[:, c * SEG:(c + 1) * SEG] = _sigmoid(seg(12 + c)).astype(g_ref.dtype)
    val = _dot(hb, w_ref[:, N_SEG * SEG:N_SEG * SEG + IF_COLS]) + ifb_ref[...]
    lane = lax.broadcasted_iota(jnp.int32, val.shape, 1)
    if_ref[...] = jnp.where(lane < H_M, val, _log_sigmoid(val))


def _proj_call(x2d, ng, w, gbd, qg, kg, ifb, *, tm, act_dtype, with_kmean):
    n_tok = x2d.shape[0]
    assert n_tok % tm == 0
    grid = (n_tok // tm,)
    row = lambda i: (i, 0)
    const = lambda i: (0, 0)
    resident = functools.partial(pl.BlockSpec, index_map=const, pipeline_mode=pl.Buffered(1))

    def tok(width, dtype):
        return pl.BlockSpec((tm, width), row), jax.ShapeDtypeStruct((n_tok, width), dtype)

    outs = [
        tok(D_A, act_dtype),
        tok(D_A, F32), tok(D_A, BF16),
        tok(D_A, F32), tok(D_A, BF16),
        tok(D_A, act_dtype),
        tok(D_M, act_dtype), tok(D_M, act_dtype), tok(D_M, act_dtype),
        tok(D_M, act_dtype),
        tok(IF_COLS, F32),
        tok(D_C, F32),
        tok(D_C, act_dtype),
        tok(N_BRANCH * D_MODEL, act_dtype),
    ]
    if with_kmean:
        assert tm % MOBA_BLOCK == 0
        nb = tm // MOBA_BLOCK
        outs.append((pl.BlockSpec((nb, 1, D_A), lambda i: (i, 0, 0)),
                     jax.ShapeDtypeStruct((n_tok // MOBA_BLOCK, 1, D_A), F32)))
    return pl.pallas_call(
        _proj_kernel,
        grid=grid,
        in_specs=[
            pl.BlockSpec((tm, D_MODEL), row),
            resident((1, D_MODEL)),
            resident((D_MODEL, N_COLS)),
            resident((D_A, D_A)),
            resident((1, D_A)),
            resident((1, D_A)),
            resident((1, IF_COLS)),
        ],
        out_specs=[o[0] for o in outs],
        out_shape=[o[1] for o in outs],
        compiler_params=_cp(("parallel",)),
        name="proj",
    )(x2d, ng, w, gbd, qg, kg, ifb)


def _out_kernel(a_ref, m_ref, c_ref, g_ref, x_ref, wa_ref, wm_ref, wc_ref, wo_ref, o_ref):
    ya = _dot(a_ref[...].astype(BF16), wa_ref[...])
    ym = _dot(m_ref[...].astype(BF16), wm_ref[...])
    yc = _dot(c_ref[...].astype(BF16), wc_ref[...])
    merged = (g_ref[:, 0:D_MODEL].astype(F32) * ya
              + g_ref[:, D_MODEL:2 * D_MODEL].astype(F32) * ym
              + g_ref[:, 2 * D_MODEL:3 * D_MODEL].astype(F32) * yc)
    o_ref[...] = x_ref[...] + _dot(merged.astype(BF16), wo_ref[...])


def _out_call(a, m, c, g, x2d, wa, wm, wc, wo, *, tm):
    n_tok = x2d.shape[0]
    row = lambda i: (i, 0)
    const = lambda i: (0, 0)
    resident = functools.partial(pl.BlockSpec, index_map=const, pipeline_mode=pl.Buffered(1))
    return pl.pallas_call(
        _out_kernel,
        grid=(n_tok // tm,),
        in_specs=[
            pl.BlockSpec((tm, D_A), row),
            pl.BlockSpec((tm, D_M), row),
            pl.BlockSpec((tm, D_C), row),
            pl.BlockSpec((tm, N_BRANCH * D_MODEL), row),
            pl.BlockSpec((tm, D_MODEL), row),
            resident((D_A, D_MODEL)),
            resident((D_M, D_MODEL)),
            resident((D_C, D_MODEL)),
            resident((D_MODEL, D_MODEL)),
        ],
        out_specs=pl.BlockSpec((tm, D_MODEL), row),
        out_shape=jax.ShapeDtypeStruct((n_tok, D_MODEL), F32),
        compiler_params=_cp(("parallel",)),
        name="outproj",
    )(a, m, c, g, x2d, wa, wm, wc, wo)


def _attn_kernel(q_ref, k_ref, vt_ref, km_ref, sza_ref, o_ref, bias_s, s_s, *, nb):
    i = pl.program_id(2)
    qpb = MOBA_BLOCK // Q_BLOCK
    own = i // qpb
    ownf = own.astype(F32)
    q2 = q_ref[...]
    lane = lax.broadcasted_iota(jnp.int32, (Q_BLOCK, LANES), 1)
    km_hi, km_mid, km_lo = _split3(km_ref[...])
    blk = lax.broadcasted_iota(jnp.int32, (nb, Q_BLOCK), 0).astype(F32)

    qs = []
    for a in range(2):
        qa = jnp.where((lane // DH_A) == a, q2, jnp.zeros_like(q2))
        gs = _dot_nt(km_hi, qa) + _dot_nt(km_mid, qa) + _dot_nt(km_lo, qa)
        gs = jnp.where(blk < ownf, gs, -jnp.inf)
        bias = jnp.where(blk == ownf, 0.0, NEG)
        for _ in range(MOBA_TOPK):
            mx = jnp.max(gs, axis=0, keepdims=True)
            cand = jnp.where((gs == mx) & (mx > -jnp.inf), blk, float(nb))
            idx = jnp.min(cand, axis=0, keepdims=True)
            pick = blk == idx
            bias = jnp.where(pick, 0.0, bias)
            gs = jnp.where(pick, -jnp.inf, gs)
        for n in range(nb):
            bias_s[a, n] = jnp.broadcast_to(bias[n:n + 1, :], (8, Q_BLOCK))
        qs.append(qa)

    def scores(j, a, kj):
        return _dot_nt(kj, qs[a]) + bias_s[a, j][0:1, :]

    def fold8(x, op):
        return op(x.reshape(MOBA_BLOCK // 8, 8, Q_BLOCK), axis=0)

    def rows(j):
        return pl.ds(pl.multiple_of(j * MOBA_BLOCK, MOBA_BLOCK), MOBA_BLOCK)

    n_full = own // ATTN_GROUP

    def pass1(g, mx):
        mx = list(mx)
        for r in range(ATTN_GROUP):
            j = g * ATTN_GROUP + r
            kj = k_ref[rows(j), :]
            for a in range(2):
                s = scores(j, a, kj)
                s_s[a, rows(j), :] = s
                mx[a] = jnp.maximum(mx[a], fold8(s, jnp.max))
        return tuple(mx)

    neg8 = jnp.full((8, Q_BLOCK), NEG, F32)
    mx = list(lax.fori_loop(0, n_full, pass1, (neg8, neg8)))
    kpos = lax.broadcasted_iota(jnp.int32, (MOBA_BLOCK, Q_BLOCK), 0)
    qpos = lax.broadcasted_iota(jnp.int32, (MOBA_BLOCK, Q_BLOCK), 1) + i * Q_BLOCK
    for r in range(ATTN_GROUP):
        j = n_full * ATTN_GROUP + r
        kj = k_ref[rows(j), :]
        visible = (kpos + j * MOBA_BLOCK) <= qpos
        for a in range(2):
            s = jnp.where(visible, scores(j, a, kj), NEG)
            s_s[a, rows(j), :] = s
            mx[a] = jnp.maximum(mx[a], fold8(s, jnp.max))
    m = [jnp.max(mx[a], axis=0, keepdims=True) for a in range(2)]

    def accumulate(j, st):
        st = list(st)
        vt = vt_ref[j]
        for a in range(2):
            p = jnp.exp(s_s[a, rows(j), :] - m[a])
            st[2 * a] = st[2 * a] + _dot(vt, p.astype(BF16))
            st[2 * a + 1] = st[2 * a + 1] + fold8(p, jnp.sum)
        return st

    def pass2(g, st):
        for r in range(ATTN_GROUP):
            st = accumulate(g * ATTN_GROUP + r, st)
        return tuple(st)

    zero = jnp.zeros((LANES, Q_BLOCK), F32)
    zero8 = jnp.zeros((8, Q_BLOCK), F32)
    st = lax.fori_loop(0, n_full, pass2, (zero, zero8, zero, zero8))
    for r in range(ATTN_GROUP):
        st = accumulate(n_full * ATTN_GROUP + r, st)
    inv = [1.0 / jnp.sum(st[2 * a + 1], axis=0, keepdims=True) for a in range(2)]
    drow = lax.broadcasted_iota(jnp.int32, (LANES, Q_BLOCK), 0)
    ot = jnp.where((drow // DH_A) == 0, st[0] * inv[0], st[2] * inv[1])
    o_ref[...] = (ot.T * sza_ref[...].astype(F32)).astype(o_ref.dtype)


def _attn_call(q, kb, vb, kmean, sza, *, batch, seq):
    assert seq % (MOBA_BLOCK * ATTN_GROUP) == 0
    nb = seq // MOBA_BLOCK
    nq = seq // Q_BLOCK
    n_pair = D_A // LANES
    vt = vb.reshape(batch, nb, MOBA_BLOCK, D_A).transpose(0, 1, 3, 2)
    return pl.pallas_call(
        functools.partial(_attn_kernel, nb=nb),
        grid=(batch, n_pair, nq),
        in_specs=[
            pl.BlockSpec((Q_BLOCK, LANES), lambda b, p, i: (b * nq + i, p)),
            pl.BlockSpec((seq, LANES), lambda b, p, i: (b, p)),
            pl.BlockSpec((None, nb, LANES, MOBA_BLOCK), lambda b, p, i: (b, 0, p, 0)),
            pl.BlockSpec((None, nb, LANES), lambda b, p, i: (b, 0, p)),
            pl.BlockSpec((Q_BLOCK, LANES), lambda b, p, i: (b * nq + i, p)),
        ],
        out_specs=pl.BlockSpec((Q_BLOCK, LANES), lambda b, p, i: (b * nq + i, p)),
        out_shape=jax.ShapeDtypeStruct((batch * seq, D_A), BF16),
        scratch_shapes=[pltpu.VMEM((2, nb, 8, Q_BLOCK), F32), pltpu.VMEM((2, seq, Q_BLOCK), F32)],
        compiler_params=_cp(("parallel", "parallel", "parallel")),
        name="moba_prompt",
    )(q, kb, vt, kmean.reshape(batch, nb, D_A), sza)


def _mlstm_kernel(q_ref, kt_ref, k_ref, v_ref, ig_ref, lf_ref, gm_ref, mg_ref,
                  h_ref, c_out, n_out, m_out, c_s, n_s, m_s):
    c = pl.program_id(2)
    L = MLSTM_CHUNK

    @pl.when(c == 0)
    def _():
        c_s[...] = jnp.zeros_like(c_s)
        n_s[...] = jnp.zeros_like(n_s)
        m_s[...] = jnp.zeros_like(m_s)

    q = q_ref[...]
    kt = kt_ref[...]
    k = k_ref[...]
    v = v_ref[...]
    ig = ig_ref[0]
    lf = lf_ref[0]
    t_i = lax.broadcasted_iota(jnp.int32, (L, L), 0)
    s_i = lax.broadcasted_iota(jnp.int32, (L, L), 1)
    causal = s_i <= t_i
    b_col = jnp.sum(jnp.where(causal, jnp.broadcast_to(lf, (L, L)), 0.0), axis=-1, keepdims=True)
    b_t = jnp.broadcast_to(b_col, (L, L))
    b_s = b_t.T
    m_prev = m_s[...]
    dmat = jnp.where(causal, b_t - b_s + jnp.broadcast_to(ig, (L, L)), NEG)
    inter = b_col + m_prev
    mt = jnp.maximum(inter, jnp.max(dmat, axis=-1, keepdims=True))
    w = jnp.exp(dmat - mt)
    dec = jnp.exp(inter - mt)
    s = _dot(q, kt) * w
    c_prev = c_s[...]
    num = _dot(s.astype(BF16), v) + dec * _dot(q, c_prev.astype(BF16))
    den = jnp.sum(s, axis=-1, keepdims=True) + dec * jnp.sum(q.astype(F32) * n_s[...], axis=-1, keepdims=True)
    hm = num / jnp.maximum(jnp.abs(den), jnp.exp(-mt))

    m_end = mt[L - 1:L, :]
    b_last = b_col[L - 1:L, :]
    b_row = b_s[0:1, :]
    dec_end = jnp.exp(b_last + m_prev - m_end)
    ws = jnp.exp(b_last - b_row + ig - m_end)
    kws = (kt.astype(F32) * ws).astype(BF16)
    c_s[...] = dec_end * c_prev + _dot(kws, v)
    ws_hi, ws_lo = _split2(jnp.broadcast_to(ws, (8, L)))
    n_s[...] = dec_end * n_s[...] + (_dot(ws_hi, k) + _dot(ws_lo, k))[0:1, :]
    m_s[...] = m_end

    hn = _layernorm_rows(hm) * mg_ref[...] * gm_ref[...].astype(F32)
    h_ref[...] = hn.astype(h_ref.dtype)

    @pl.when(c == pl.num_programs(2) - 1)
    def _():
        c_out[...] = c_s[...]
        n_out[...] = n_s[...]
        m_out[...] = jnp.broadcast_to(m_s[...], m_out.shape)


def _mlstm_call(qm, km, vm, igf, gm, mg, *, batch, seq):
    L = MLSTM_CHUNK
    assert seq % L == 0
    nc = seq // L
    kt = km.reshape(batch, seq, D_M).transpose(0, 2, 1)
    gates = igf[:, :2 * H_M].reshape(batch, nc, L, 2 * H_M).transpose(0, 3, 1, 2)
    ig = gates[:, :H_M].reshape(batch * H_M * nc, 1, L)
    lf = gates[:, H_M:].reshape(batch * H_M * nc, 1, L)
    tile = pl.BlockSpec((L, DH_M), lambda b, h, c: (b * nc + c, h))
    gate = pl.BlockSpec((1, 1, L), lambda b, h, c: ((b * H_M + h) * nc + c, 0, 0))
    state = lambda r: pl.BlockSpec((None, None, r, DH_M), lambda b, h, c: (b, h, 0, 0))
    return pl.pallas_call(
        _mlstm_kernel,
        grid=(batch, H_M, nc),
        in_specs=[
            tile,
            pl.BlockSpec((None, DH_M, L), lambda b, h, c: (b, h, c)),
            tile, tile, gate, gate, tile,
            pl.BlockSpec((1, DH_M), lambda b, h, c: (0, h)),
        ],
        out_specs=[tile, state(DH_M), state(1), state(1)],
        out_shape=[
            jax.ShapeDtypeStruct((batch * seq, D_M), BF16),
            jax.ShapeDtypeStruct((batch, H_M, DH_M, DH_M), F32),
            jax.ShapeDtypeStruct((batch, H_M, 1, DH_M), F32),
            jax.ShapeDtypeStruct((batch, H_M, 1, DH_M), F32),
        ],
        scratch_shapes=[pltpu.VMEM((DH_M, DH_M), F32), pltpu.VMEM((1, DH_M), F32), pltpu.VMEM((1, 1), F32)],
        compiler_params=_cp(("parallel", "parallel", "arbitrary")),
        name="mlstm_prompt",
    )(qm, kt, km, vm, ig, lf, gm, mg)


def _conv_epilogue(c, lg_ref, lb_ref, szc_ref, o_ref):
    y = _layernorm_rows(c) * lg_ref[...] + lb_ref[...]
    o_ref[...] = (_silu(y) * szc_ref[...].astype(F32)).astype(o_ref.dtype)


def _conv_kernel(u_ref, halo_ref, w_ref, cb_ref, lg_ref, lb_ref, szc_ref, o_ref, full):
    i = pl.program_id(1)
    tm = u_ref.shape[0]
    full[0:HALO, :] = jnp.where(i > 0, halo_ref[...], 0.0)
    full[HALO:HALO + tm, :] = u_ref[...]
    first = HALO - (CONV_W - 1)
    acc = jnp.zeros((tm, D_C), F32)
    for j in range(CONV_W):
        acc = acc + w_ref[j:j + 1, :] * full[first + j:first + j + tm, :]
    _conv_epilogue(acc + cb_ref[...], lg_ref, lb_ref, szc_ref, o_ref)


def _conv_call(u, wpad, cb, lg, lb, szc, *, batch, seq, tm):
    assert seq % tm == 0 and tm % HALO == 0
    nt = seq // tm
    row = lambda b, i: (b * nt + i, 0)
    const = lambda b, i: (0, 0)
    halo_idx = lambda b, i: (jnp.maximum((b * nt + i) * (tm // HALO) - 1, 0), 0)
    return pl.pallas_call(
        _conv_kernel,
        grid=(batch, nt),
        in_specs=[
            pl.BlockSpec((tm, D_C), row),
            pl.BlockSpec((HALO, D_C), halo_idx),
            pl.BlockSpec((HALO, D_C), const),
            pl.BlockSpec((1, D_C), const),
            pl.BlockSpec((1, D_C), const),
            pl.BlockSpec((1, D_C), const),
            pl.BlockSpec((tm, D_C), row),
        ],
        out_specs=pl.BlockSpec((tm, D_C), row),
        out_shape=jax.ShapeDtypeStruct((batch * seq, D_C), BF16),
        scratch_shapes=[pltpu.VMEM((HALO + tm, D_C), F32)],
        compiler_params=_cp(("parallel", "parallel")),
        name="conv_prompt",
    )(u, u, wpad, cb, lg, lb, szc)


def _attn_sample_kernel(pt_ref, q_ref, kn_ref, vn_ref, sza_ref, *refs, n_pages):
    del pt_ref
    k_refs = refs[:n_pages]
    v_refs = refs[n_pages:2 * n_pages]
    o_ref = refs[2 * n_pages]
    s_s = refs[2 * n_pages + 1]
    n_blk = n_pages * PAGE_SIZE // MOBA_BLOCK
    ppb = MOBA_BLOCK // PAGE_SIZE
    q = q_ref[...]

    colsum = []
    for pg in range(n_pages):
        kp = k_refs[pg][...]
        s_s[pg] = jnp.sum(kp * q, axis=-1, keepdims=True)
        colsum.append(jnp.sum(kp, axis=0))

    gs = []
    for j in range(n_blk):
        tot = colsum[j * ppb]
        for r in range(1, ppb):
            tot = tot + colsum[j * ppb + r]
        gs.append(jnp.sum(q * (tot * (1.0 / MOBA_BLOCK)), axis=-1, keepdims=True))

    picked = [jnp.zeros((H_A, 1), F32) for _ in range(n_blk)]
    for _ in range(min(MOBA_TOPK, n_blk)):
        mx = gs[0]
        for j in range(1, n_blk):
            mx = jnp.maximum(mx, gs[j])
        found = jnp.zeros((H_A, 1), F32)
        for j in range(n_blk):
            hit = jnp.where((gs[j] == mx) & (found == 0.0), 1.0, 0.0)
            picked[j] = jnp.maximum(picked[j], hit)
            found = jnp.maximum(found, hit)
            gs[j] = jnp.where(hit > 0.0, -jnp.inf, gs[j])

    s_own = jnp.sum(q * kn_ref[...], axis=-1, keepdims=True)
    m = s_own
    shift = []
    for pg in range(n_pages):
        bias = jnp.where(picked[pg // ppb] > 0.0, 0.0, NEG)
        shift.append(bias)
        m = jnp.maximum(m, jnp.max(s_s[pg], axis=0) + bias)
    p_own = jnp.exp(s_own - m)
    l = p_own
    o = p_own * vn_ref[...]
    for pg in range(n_pages):
        p = jnp.exp(s_s[pg] + (shift[pg] - m))
        l = l + jnp.sum(p, axis=0)
        o = o + jnp.sum(p * v_refs[pg][...], axis=0)
    o_ref[...] = o / l * sza_ref[...]


def _attn_sample_call(page_table, q, kn, vn, sza, cache_k, cache_v, layer):
    db, n_pages = page_table.shape
    assert MOBA_BLOCK % PAGE_SIZE == 0 and (n_pages * PAGE_SIZE) % MOBA_BLOCK == 0
    assert cache_k.shape[2:] == (PAGE_SIZE, H_A, DH_A)
    headspec = pl.BlockSpec((None, H_A, DH_A), lambda b, pt: (b, 0, 0))

    def page(pg):
        return pl.BlockSpec((None, None, PAGE_SIZE, H_A, DH_A), lambda b, pt: (pt[b, pg], layer, 0, 0, 0))

    grid_spec = pltpu.PrefetchScalarGridSpec(
        num_scalar_prefetch=1,
        grid=(db,),
        in_specs=[headspec] * 4 + [page(pg) for pg in range(n_pages)] * 2,
        out_specs=headspec,
        scratch_shapes=[pltpu.VMEM((n_pages, PAGE_SIZE, H_A, 1), F32)],
    )
    r3 = lambda t: t.reshape(db, H_A, DH_A)
    return pl.pallas_call(
        functools.partial(_attn_sample_kernel, n_pages=n_pages),
        grid_spec=grid_spec,
        out_shape=jax.ShapeDtypeStruct((db, H_A, DH_A), F32),
        compiler_params=_cp(("parallel",)),
        name="moba_sample",
    )(page_table, r3(q), r3(kn), r3(vn), r3(sza), *([cache_k] * n_pages), *([cache_v] * n_pages)).reshape(db, D_A)


def _mlstm_sample_kernel(q_ref, k_ref, v_ref, if_ref, gm_ref, mg_ref, c_ref, n_ref, m_ref,
                         h_ref, c_out, n_out, m_out):
    bb = q_ref.shape[0]

    def body(bi, carry):
        for h in range(H_M):
            sl = slice(h * DH_M, (h + 1) * DH_M)
            q = q_ref[bi, :, sl]
            k = k_ref[bi, :, sl]
            v = v_ref[bi, :, sl]
            ig = if_ref[bi, :, h:h + 1]
            lf = if_ref[bi, :, H_M + h:H_M + h + 1]
            m0 = m_ref[bi, :, h:h + 1]
            c0 = c_ref[bi, h]
            n0 = n_ref[bi, h:h + 1, :]
            q_cols = jnp.broadcast_to(q, (DH_M, DH_M)).T
            k_cols = jnp.broadcast_to(k, (DH_M, DH_M)).T
            inter = lf + m0
            mt = jnp.maximum(inter, ig)
            w = jnp.exp(ig - mt)
            dec = jnp.exp(inter - mt)
            s = jnp.sum(q * k, axis=-1, keepdims=True) * w
            qc = jnp.sum(q_cols * c0, axis=0, keepdims=True)
            num = s * v + dec * qc
            den = s + dec * jnp.sum(q * n0, axis=-1, keepdims=True)
            hm = num / jnp.maximum(jnp.abs(den), jnp.exp(-mt))
            c_out[bi, h] = dec * c0 + w * (k_cols * v)
            n_out[bi, h:h + 1, :] = dec * n0 + w * k
            m_out[bi, :, h:h + 1] = mt
            h_ref[bi, :, sl] = _layernorm_rows(hm) * mg_ref[:, sl] * gm_ref[bi, :, sl]
        return carry

    lax.fori_loop(0, bb, body, 0)


def _mlstm_sample_call(qm, km, vm, igf, gm, mg, c0, n0, m0, layer, *, bb):
    db = qm.shape[0]
    assert db % bb == 0
    depth = m0.shape[0]
    row3 = lambda i: (i, 0, 0)
    r3 = lambda t: t.reshape(db, 1, t.shape[-1])
    tokspec = pl.BlockSpec((bb, 1, D_M), row3)
    h3, c1, n1, m1 = pl.pallas_call(
        _mlstm_sample_kernel,
        grid=(db // bb,),
        in_specs=[
            tokspec, tokspec, tokspec,
            pl.BlockSpec((bb, 1, IF_COLS), row3),
            tokspec,
            pl.BlockSpec((1, D_M), lambda i: (0, 0)),
            pl.BlockSpec((None, bb, H_M, DH_M, DH_M), lambda i: (layer, i, 0, 0, 0)),
            pl.BlockSpec((None, bb, H_M, DH_M), lambda i: (layer, i, 0, 0)),
            pl.BlockSpec((None, bb, 1, H_M), lambda i: (layer, i, 0, 0)),
        ],
        out_specs=[
            tokspec,
            pl.BlockSpec((bb, H_M, DH_M, DH_M), lambda i: (i, 0, 0, 0)),
            pl.BlockSpec((bb, H_M, DH_M), row3),
            pl.BlockSpec((bb, 1, H_M), row3),
        ],
        out_shape=[
            jax.ShapeDtypeStruct((db, 1, D_M), F32),
            jax.ShapeDtypeStruct((db, H_M, DH_M, DH_M), F32),
            jax.ShapeDtypeStruct((db, H_M, DH_M), F32),
            jax.ShapeDtypeStruct((db, 1, H_M), F32),
        ],
        compiler_params=_cp(("parallel",)),
        name="mlstm_sample",
    )(r3(qm), r3(km), r3(vm), r3(igf), r3(gm), mg, c0, n0, m0.reshape(depth, db, 1, H_M))
    return h3.reshape(db, D_M), c1, n1, m1.reshape(db, H_M)


def _conv_sample_kernel(st_ref, u_ref, w_ref, cb_ref, lg_ref, lb_ref, szc_ref, o_ref):
    acc = w_ref[CONV_W - 1:CONV_W, :] * u_ref[...]
    for j in range(CONV_W - 1):
        acc = acc + w_ref[j:j + 1, :] * st_ref[j]
    _conv_epilogue(acc + cb_ref[...], lg_ref, lb_ref, szc_ref, o_ref)


def _conv_sample_call(state_t, u, wpad, cb, lg, lb, szc, *, bb):
    db = u.shape[0]
    row = lambda i: (i, 0)
    const = lambda i: (0, 0)
    return pl.pallas_call(
        _conv_sample_kernel,
        grid=(db // bb,),
        in_specs=[
            pl.BlockSpec((CONV_W - 1, bb, D_C), lambda i: (0, i, 0)),
            pl.BlockSpec((bb, D_C), row),
            pl.BlockSpec((HALO, D_C), const),
            pl.BlockSpec((1, D_C), const),
            pl.BlockSpec((1, D_C), const),
            pl.BlockSpec((1, D_C), const),
            pl.BlockSpec((bb, D_C), row),
        ],
        out_specs=pl.BlockSpec((bb, D_C), row),
        out_shape=jax.ShapeDtypeStruct((db, D_C), F32),
        compiler_params=_cp(("parallel",)),
        name="conv_sample",
    )(state_t, u, wpad, cb, lg, lb, szc)


def _prep_weights(norm_g, w_in, b_igate, b_fgate, qn_g, kn_g, w_a_out, mh_norm_g, w_m_out,
                  conv_w, conv_b, cln_g, cln_b, w_c_out, w_out):
    depth = w_in.shape[0]
    gate0 = 4 * D_A + 5 * D_M
    w = jnp.concatenate([
        w_in[:, :, :gate0], w_in[:, :, gate0 + 2 * H_M:], w_in[:, :, gate0:gate0 + 2 * H_M],
        jnp.zeros((depth, D_MODEL, IF_COLS - 2 * H_M), w_in.dtype)], axis=-1).astype(BF16)
    assert w.shape[-1] == N_COLS
    ifb = jnp.concatenate([b_igate, b_fgate, jnp.zeros((depth, IF_COLS - 2 * H_M), F32)], axis=-1)
    gbd = jnp.asarray(np.kron(np.eye(H_A), np.ones((DH_A, DH_A))), BF16)
    wpad = jnp.concatenate([conv_w, jnp.zeros((depth, HALO - CONV_W, D_C), conv_w.dtype)], axis=1)
    layers = []
    for l in range(depth):
        layers.append(dict(
            ng=norm_g[l][None], w=w[l], gbd=gbd, ifb=ifb[l][None],
            qg=jnp.tile(qn_g[l], H_A)[None], kg=jnp.tile(kn_g[l], H_A)[None],
            wa=w_a_out[l].astype(BF16), wm=w_m_out[l].astype(BF16), wc=w_c_out[l].astype(BF16),
            wo=w_out[l].astype(BF16), mg=mh_norm_g[l][None], wpad=wpad[l],
            cb=conv_b[l][None], lg=cln_g[l][None], lb=cln_b[l][None]))
    return layers


def _prompt_layer(x2d, p, *, batch, seq):
    (q, kf, kb, vf, vb, sza, qm, km, vm, gm, igf, u, szc, g, kmean) = _proj_call(
        x2d, p["ng"], p["w"], p["gbd"], p["qg"], p["kg"], p["ifb"],
        tm=MOBA_BLOCK, act_dtype=BF16, with_kmean=True)
    a_in = _attn_call(q, kb, vb, kmean, sza, batch=batch, seq=seq)
    m_in, c1, n1, m1 = _mlstm_call(qm, km, vm, igf, gm, p["mg"], batch=batch, seq=seq)
    c_in = _conv_call(u, p["wpad"], p["cb"], p["lg"], p["lb"], szc, batch=batch, seq=seq, tm=MOBA_BLOCK)
    y = _out_call(a_in, m_in, c_in, g, x2d, p["wa"], p["wm"], p["wc"], p["wo"], tm=MOBA_BLOCK)
    conv_state = u.reshape(batch, seq, D_C)[:, seq - (CONV_W - 1):]
    return y, kf, vf, c1, n1[:, :, 0], m1[:, :, 0, 0], conv_state


def _sample_layer(x2d, p, page_table, cache_k, cache_v, st_c, st_n, st_m, st_conv, layer):
    db = x2d.shape[0]
    (q, kf, _, vf, _, sza, qm, km, vm, gm, igf, u, szc, g) = _proj_call(
        x2d, p["ng"], p["w"], p["gbd"], p["qg"], p["kg"], p["ifb"],
        tm=db, act_dtype=F32, with_kmean=False)
    a_in = _attn_sample_call(page_table, q, kf, vf, sza, cache_k, cache_v, layer)
    m_in, c1, n1, m1 = _mlstm_sample_call(qm, km, vm, igf, gm, p["mg"], st_c, st_n, st_m, layer, bb=8)
    conv_hist = st_conv[layer]
    c_in = _conv_sample_call(conv_hist.transpose(1, 0, 2), u, p["wpad"], p["cb"], p["lg"], p["lb"], szc, bb=8)
    y = _out_call(a_in, m_in, c_in, g, x2d, p["wa"], p["wm"], p["wc"], p["wo"], tm=db)
    conv_state = jnp.concatenate([conv_hist[:, 1:], u[:, None, :]], axis=1)
    return y, kf, vf, c1, n1, m1, conv_state


def kernel(x_prompt, x_sample, cache_k, cache_v, state_mlstm_C, state_mlstm_n, state_mlstm_m, state_conv,
           page_table, norm_g, w_in, b_igate, b_fgate, qn_g, kn_g, w_a_out, mh_norm_g, w_m_out,
           conv_w, conv_b, cln_g, cln_b, w_c_out, w_out):
    batch, seq, _ = x_prompt.shape
    db, dseq, _ = x_sample.shape
    assert dseq == 1
    depth = w_in.shape[0]
    layers = _prep_weights(norm_g, w_in, b_igate, b_fgate, qn_g, kn_g, w_a_out, mh_norm_g, w_m_out,
                           conv_w, conv_b, cln_g, cln_b, w_c_out, w_out)
    yp = x_prompt.reshape(batch * seq, D_MODEL)
    ys = x_sample.reshape(db, D_MODEL)
    pr, sa = [], []
    for l in range(depth):
        yp, *rest = _prompt_layer(yp, layers[l], batch=batch, seq=seq)
        pr.append(rest)
        ys, *rest = _sample_layer(ys, layers[l], page_table, cache_k, cache_v,
                                  state_mlstm_C, state_mlstm_n, state_mlstm_m, state_conv, l)
        sa.append(rest)

    def stack(rows, idx, axis):
        return jnp.stack([r[idx] for r in rows], axis=axis)

    k_prompt = jnp.stack([r[0].reshape(batch, seq, H_A, DH_A) for r in pr], axis=1)
    v_prompt = jnp.stack([r[1].reshape(batch, seq, H_A, DH_A) for r in pr], axis=1)
    k_sample = stack(sa, 0, 1).reshape(db, depth, 1, H_A, DH_A)
    v_sample = stack(sa, 1, 1).reshape(db, depth, 1, H_A, DH_A)
    return (yp.reshape(batch, seq, D_MODEL), ys.reshape(db, 1, D_MODEL),
            k_prompt, v_prompt, stack(pr, 2, 0), stack(pr, 3, 0), stack(pr, 4, 0), stack(pr, 5, 0),
            k_sample, v_sample, stack(sa, 2, 0), stack(sa, 3, 0), stack(sa, 4, 0), stack(sa, 5, 0))
```

```python
import functools

import numpy as np
import jax
import jax.numpy as jnp
from jax import lax
from jax.experimental import pallas as pl
from jax.experimental.pallas import tpu as pltpu

F32 = jnp.float32
BF16 = jnp.bfloat16

D_MODEL = 1024
H_A = 8
DH_A = 64
D_A = H_A * DH_A
MOBA_BLOCK = 256
MOBA_TOPK = 3
ATTN_Q = 256
H_M = 4
DH_M = 128
D_M = H_M * DH_M
MLSTM_CHUNK = 128
D_C = 512
CONV_W = 31
N_BRANCH = 3
PAGE_SIZE = 128
EPS = 1e-6
NEG = -1e30
LOG2E = 1.4426950408889634

SEG = 512
N_SEG = 12 + N_BRANCH * D_MODEL // SEG
IF_COLS = 128
N_COLS = N_SEG * SEG + IF_COLS
HALO = 32
LANES = 128
ATTN_GROUP = 2

_NT = (((1,), (1,)), ((), ()))

_VMEM_LIMIT = 52 * 1024 * 1024


def _cp(sem, vmem=_VMEM_LIMIT):
    return pltpu.CompilerParams(dimension_semantics=sem, vmem_limit_bytes=vmem)


def _sigmoid(x):
    return 1.0 / (1.0 + jnp.exp(-x))


def _silu(x):
    return x * _sigmoid(x)


def _log_sigmoid(x):
    return jnp.minimum(x, 0.0) - jnp.log(1.0 + jnp.exp(-jnp.abs(x)))


def _dot(a, b):
    return jnp.dot(a, b, preferred_element_type=F32)


def _dot_nt(a, b):
    return lax.dot_general(a, b, _NT, preferred_element_type=F32)


def _split2(x):
    hi = x.astype(BF16)
    lo = (x - hi.astype(F32)).astype(BF16)
    return hi, lo


def _split3(x):
    hi = x.astype(BF16)
    r = x - hi.astype(F32)
    mid = r.astype(BF16)
    lo = (r - mid.astype(F32)).astype(BF16)
    return hi, mid, lo


def _layernorm_rows(x):
    mu = jnp.mean(x, axis=-1, keepdims=True)
    xc = x - mu
    var = jnp.mean(xc * xc, axis=-1, keepdims=True)
    return xc * lax.rsqrt(var + EPS)


def _proj_kernel(x_ref, ng_ref, w_ref, gbd_ref, qg_ref, kg_ref, ifb_ref,
                 q_ref, kf_ref, kb_ref, vf_ref, vb_ref, sza_ref, qm_ref, km_ref, vm_ref,
                 gm_ref, if_ref, u_ref, szc_ref, g_ref, *kmean_refs):
    x = x_ref[...]
    h = x * lax.rsqrt(jnp.mean(x * x, axis=-1, keepdims=True) + EPS) * ng_ref[...]
    hb = h.astype(BF16)

    def seg(i):
        return _dot(hb, w_ref[:, i * SEG:(i + 1) * SEG])

    def headnorm(t, g_row):
        hi, lo = _split2(t * t)
        ms = (_dot(hi, gbd_ref[...]) + _dot(lo, gbd_ref[...])) * (1.0 / DH_A)
        return t * lax.rsqrt(ms + EPS) * g_row

    qn = headnorm(seg(0), qg_ref[...])
    q_ref[...] = (qn * (DH_A ** -0.5 * LOG2E)).astype(q_ref.dtype)
    kn = headnorm(seg(1), kg_ref[...])
    kf_ref[...] = kn
    kb_ref[...] = kn.astype(BF16)
    if kmean_refs:
        kmean_ref = kmean_refs[0]
        for r in range(kmean_ref.shape[0]):
            kmean_ref[r] = jnp.mean(kn[r * MOBA_BLOCK:(r + 1) * MOBA_BLOCK], axis=0, keepdims=True)
    v = seg(2)
    vf_ref[...] = v
    vb_ref[...] = v.astype(BF16)
    sza_ref[...] = _silu(seg(3)).astype(sza_ref.dtype)
    qm_ref[...] = seg(4).astype(qm_ref.dtype)
    km_ref[...] = (seg(5) * (DH_M ** -0.5)).astype(km_ref.dtype)
    vm_ref[...] = seg(6).astype(vm_ref.dtype)
    gm_ref[...] = (_sigmoid(seg(7)) * _silu(seg(8))).astype(gm_ref.dtype)
    u_ref[...] = seg(9) * _sigmoid(seg(10))
    szc_ref[...] = _silu(seg(11)).astype(szc_ref.dtype)
    for c in range(N_BRANCH * D_MODEL // SEG):
        g_ref[:, c * SEG:(c + 1) * SEG] = _sigmoid(seg(12 + c)).astype(g_ref.dtype)
    val = _dot(hb, w_ref[:, N_SEG * SEG:N_SEG * SEG + IF_COLS]) + ifb_ref[...]
    lane = lax.broadcasted_iota(jnp.int32, val.shape, 1)
    if_ref[...] = jnp.where(lane < H_M, val, _log_sigmoid(val))


def _proj_call(x2d, ng, w, gbd, qg, kg, ifb, *, tm, act_dtype, with_kmean):
    n_tok = x2d.shape[0]
    assert n_tok % tm == 0
    grid = (n_tok // tm,)
    row = lambda i: (i, 0)
    const = lambda i: (0, 0)
    resident = functools.partial(pl.BlockSpec, index_map=const, pipeline_mode=pl.Buffered(1))

    def tok(width, dtype):
        return pl.BlockSpec((tm, width), row), jax.ShapeDtypeStruct((n_tok, width), dtype)

    outs = [
        tok(D_A, act_dtype),
        tok(D_A, F32), tok(D_A, BF16),
        tok(D_A, F32), tok(D_A, BF16),
        tok(D_A, act_dtype),
        tok(D_M, act_dtype), tok(D_M, act_dtype), tok(D_M, act_dtype),
        tok(D_M, act_dtype),
        tok(IF_COLS, F32),
        tok(D_C, F32),
        tok(D_C, act_dtype),
        tok(N_BRANCH * D_MODEL, act_dtype),
    ]
    if with_kmean:
        assert tm % MOBA_BLOCK == 0
        nb = tm // MOBA_BLOCK
        outs.append((pl.BlockSpec((nb, 1, D_A), lambda i: (i, 0, 0)),
                     jax.ShapeDtypeStruct((n_tok // MOBA_BLOCK, 1, D_A), F32)))
    return pl.pallas_call(
        _proj_kernel,
        grid=grid,
        in_specs=[
            pl.BlockSpec((tm, D_MODEL), row),
            resident((1, D_MODEL)),
            resident((D_MODEL, N_COLS)),
            resident((D_A, D_A)),
            resident((1, D_A)),
            resident((1, D_A)),
            resident((1, IF_COLS)),
        ],
        out_specs=[o[0] for o in outs],
        out_shape=[o[1] for o in outs],
        compiler_params=_cp(("parallel",)),
        name="proj",
    )(x2d, ng, w, gbd, qg, kg, ifb)


def _out_kernel(a_ref, m_ref, c_ref, g_ref, x_ref, wa_ref, wm_ref, wc_ref, wo_ref, o_ref):
    ya = _dot(a_ref[...].astype(BF16), wa_ref[...])
    ym = _dot(m_ref[...].astype(BF16), wm_ref[...])
    yc = _dot(c_ref[...].astype(BF16), wc_ref[...])
    merged = (g_ref[:, 0:D_MODEL].astype(F32) * ya
              + g_ref[:, D_MODEL:2 * D_MODEL].astype(F32) * ym
              + g_ref[:, 2 * D_MODEL:3 * D_MODEL].astype(F32) * yc)
    o_ref[...] = x_ref[...] + _dot(merged.astype(BF16), wo_ref[...])


def _out_call(a, m, c, g, x2d, wa, wm, wc, wo, *, tm):
    n_tok = x2d.shape[0]
    row = lambda i: (i, 0)
    const = lambda i: (0, 0)
    resident = functools.partial(pl.BlockSpec, index_map=const, pipeline_mode=pl.Buffered(1))
    return pl.pallas_call(
        _out_kernel,
        grid=(n_tok // tm,),
        in_specs=[
            pl.BlockSpec((tm, D_A), row),
            pl.BlockSpec((tm, D_M), row),
            pl.BlockSpec((tm, D_C), row),
            pl.BlockSpec((tm, N_BRANCH * D_MODEL), row),
            pl.BlockSpec((tm, D_MODEL), row),
            resident((D_A, D_MODEL)),
            resident((D_M, D_MODEL)),
            resident((D_C, D_MODEL)),
            resident((D_MODEL, D_MODEL)),
        ],
        out_specs=pl.BlockSpec((tm, D_MODEL), row),
        out_shape=jax.ShapeDtypeStruct((n_tok, D_MODEL), F32),
        compiler_params=_cp(("parallel",)),
        name="outproj",
    )(a, m, c, g, x2d, wa, wm, wc, wo)


def _attn_kernel(q_ref, k_ref, vt0_ref, vt1_ref, km_ref, sza_ref, o_ref, bias_s, s_s, *, nb):
    vt_refs = (vt0_ref, vt1_ref)
    i = pl.program_id(2)
    own = (i * ATTN_Q) // MOBA_BLOCK
    ownf = own.astype(F32)
    q2 = q_ref[...]
    lane = lax.broadcasted_iota(jnp.int32, (ATTN_Q, LANES), 1)
    km_hi, km_mid, km_lo = _split3(km_ref[...])
    blk = lax.broadcasted_iota(jnp.int32, (nb, ATTN_Q), 0).astype(F32)

    qs = []
    for a in range(2):
        qa = jnp.where((lane // DH_A) == a, q2, jnp.zeros_like(q2))
        gs = _dot_nt(km_hi, qa) + _dot_nt(km_mid, qa) + _dot_nt(km_lo, qa)
        gs = jnp.where(blk < ownf, gs, -jnp.inf)
        bias = jnp.where(blk == ownf, 0.0, NEG)
        for _ in range(MOBA_TOPK):
            mx = jnp.max(gs, axis=0, keepdims=True)
            cand = jnp.where((gs == mx) & (mx > -jnp.inf), blk, float(nb))
            idx = jnp.min(cand, axis=0, keepdims=True)
            pick = blk == idx
            bias = jnp.where(pick, 0.0, bias)
            gs = jnp.where(pick, -jnp.inf, gs)
        for n in range(nb):
            bias_s[a, n] = jnp.broadcast_to(bias[n:n + 1, :], (8, ATTN_Q))
        qs.append(qa)

    def scores(j, a, kj):
        return _dot_nt(kj, qs[a]) + bias_s[a, j][0:1, :]

    def fold8(x, op):
        return op(x.reshape(MOBA_BLOCK // 8, 8, ATTN_Q), axis=0)

    def rows(j):
        return pl.ds(pl.multiple_of(j * MOBA_BLOCK, MOBA_BLOCK), MOBA_BLOCK)

    n_full = own // ATTN_GROUP

    def pass1(g, mx):
        mx = list(mx)
        for r in range(ATTN_GROUP):
            j = g * ATTN_GROUP + r
            kj = k_ref[rows(j), :]
            for a in range(2):
                s = scores(j, a, kj)
                s_s[a, rows(j), :] = s
                mx[a] = jnp.maximum(mx[a], fold8(s, jnp.max))
        return tuple(mx)

    neg8 = jnp.full((8, ATTN_Q), NEG, F32)
    mx = list(lax.fori_loop(0, n_full, pass1, (neg8, neg8)))
    kpos = lax.broadcasted_iota(jnp.int32, (MOBA_BLOCK, ATTN_Q), 0)
    qpos = lax.broadcasted_iota(jnp.int32, (MOBA_BLOCK, ATTN_Q), 1) + i * ATTN_Q
    for r in range(ATTN_GROUP):
        j = n_full * ATTN_GROUP + r
        kj = k_ref[rows(j), :]
        visible = (kpos + j * MOBA_BLOCK) <= qpos
        for a in range(2):
            s = jnp.where(visible, scores(j, a, kj), NEG)
            s_s[a, rows(j), :] = s
            mx[a] = jnp.maximum(mx[a], fold8(s, jnp.max))
    m = [jnp.max(mx[a], axis=0, keepdims=True) for a in range(2)]

    def accumulate(j, acc):
        return [acc[a] + _dot(vt_refs[a][j], jnp.exp2(s_s[a, rows(j), :] - m[a]).astype(BF16))
                for a in range(2)]

    def pass2(g, acc):
        for r in range(ATTN_GROUP):
            acc = accumulate(g * ATTN_GROUP + r, acc)
        return tuple(acc)

    zero = jnp.zeros((LANES, ATTN_Q), F32)
    acc = lax.fori_loop(0, n_full, pass2, (zero, zero))
    for r in range(ATTN_GROUP):
        acc = accumulate(n_full * ATTN_GROUP + r, acc)
    inv0 = 1.0 / acc[0][DH_A:DH_A + 1, :]
    inv1 = 1.0 / acc[1][0:1, :]
    drow = lax.broadcasted_iota(jnp.int32, (LANES, ATTN_Q), 0)
    ot = jnp.where((drow // DH_A) == 0, acc[0] * inv0, acc[1] * inv1)
    o_ref[...] = (ot.T * sza_ref[...].astype(F32)).astype(o_ref.dtype)


def _attn_call(q, kb, vb, kmean, sza, *, batch, seq):
    assert seq % (MOBA_BLOCK * ATTN_GROUP) == 0 and MOBA_BLOCK % ATTN_Q == 0
    nb = seq // MOBA_BLOCK
    nq = seq // ATTN_Q
    n_pair = D_A // LANES
    vt = vb.reshape(batch, nb, MOBA_BLOCK, D_A).transpose(0, 1, 3, 2)
    head_in_pair = (lax.broadcasted_iota(jnp.int32, (1, 1, D_A, 1), 2) // DH_A) % 2
    vts = [jnp.where(head_in_pair == a, vt, jnp.ones_like(vt)) for a in range(2)]
    vt_spec = pl.BlockSpec((None, nb, LANES, MOBA_BLOCK), lambda b, p, i: (b, 0, p, 0))
    return pl.pallas_call(
        functools.partial(_attn_kernel, nb=nb),
        grid=(batch, n_pair, nq),
        in_specs=[
            pl.BlockSpec((ATTN_Q, LANES), lambda b, p, i: (b * nq + i, p)),
            pl.BlockSpec((seq, LANES), lambda b, p, i: (b, p)),
            vt_spec, vt_spec,
            pl.BlockSpec((None, nb, LANES), lambda b, p, i: (b, 0, p)),
            pl.BlockSpec((ATTN_Q, LANES), lambda b, p, i: (b * nq + i, p)),
        ],
        out_specs=pl.BlockSpec((ATTN_Q, LANES), lambda b, p, i: (b * nq + i, p)),
        out_shape=jax.ShapeDtypeStruct((batch * seq, D_A), BF16),
        scratch_shapes=[pltpu.VMEM((2, nb, 8, ATTN_Q), F32), pltpu.VMEM((2, seq, ATTN_Q), F32)],
        compiler_params=_cp(("parallel", "parallel", "parallel")),
        name="moba_prompt",
    )(q, kb, vts[0], vts[1], kmean.reshape(batch, nb, D_A), sza)


def _mlstm_kernel(q_ref, kt_ref, k_ref, v_ref, ig_ref, lf_ref, gm_ref, mg_ref,
                  h_ref, c_out, n_out, m_out, c_s, n_s, m_s):
    c = pl.program_id(2)
    L = MLSTM_CHUNK

    @pl.when(c == 0)
    def _():
        c_s[...] = jnp.zeros_like(c_s)
        n_s[...] = jnp.zeros_like(n_s)
        m_s[...] = jnp.zeros_like(m_s)

    q = q_ref[...]
    kt = kt_ref[...]
    k = k_ref[...]
    v = v_ref[...]
    ig = ig_ref[0]
    lf = lf_ref[0]
    t_i = lax.broadcasted_iota(jnp.int32, (L, L), 0)
    s_i = lax.broadcasted_iota(jnp.int32, (L, L), 1)
    causal = s_i <= t_i
    b_col = jnp.sum(jnp.where(causal, jnp.broadcast_to(lf, (L, L)), 0.0), axis=-1, keepdims=True)
    b_t = jnp.broadcast_to(b_col, (L, L))
    b_s = b_t.T
    m_prev = m_s[...]
    dmat = jnp.where(causal, b_t - b_s + jnp.broadcast_to(ig, (L, L)), NEG)
    inter = b_col + m_prev
    mt = jnp.maximum(inter, jnp.max(dmat, axis=-1, keepdims=True))
    w = jnp.exp(dmat - mt)
    dec = jnp.exp(inter - mt)
    s = _dot(q, kt) * w
    c_prev = c_s[...]
    num = _dot(s.astype(BF16), v) + dec * _dot(q, c_prev.astype(BF16))
    den = jnp.sum(s, axis=-1, keepdims=True) + dec * jnp.sum(q.astype(F32) * n_s[...], axis=-1, keepdims=True)
    hm = num / jnp.maximum(jnp.abs(den), jnp.exp(-mt))

    m_end = mt[L - 1:L, :]
    b_last = b_col[L - 1:L, :]
    b_row = b_s[0:1, :]
    dec_end = jnp.exp(b_last + m_prev - m_end)
    ws = jnp.exp(b_last - b_row + ig - m_end)
    kws = (kt.astype(F32) * ws).astype(BF16)
    c_s[...] = dec_end * c_prev + _dot(kws, v)
    ws_hi, ws_lo = _split2(jnp.broadcast_to(ws, (8, L)))
    n_s[...] = dec_end * n_s[...] + (_dot(ws_hi, k) + _dot(ws_lo, k))[0:1, :]
    m_s[...] = m_end

    hn = _layernorm_rows(hm) * mg_ref[...] * gm_ref[...].astype(F32)
    h_ref[...] = hn.astype(h_ref.dtype)

    @pl.when(c == pl.num_programs(2) - 1)
    def _():
        c_out[...] = c_s[...]
        n_out[...] = n_s[...]
        m_out[...] = jnp.broadcast_to(m_s[...], m_out.shape)


def _mlstm_call(qm, km, vm, igf, gm, mg, *, batch, seq):
    L = MLSTM_CHUNK
    assert seq % L == 0
    nc = seq // L
    kt = km.reshape(batch, seq, D_M).transpose(0, 2, 1)
    gates = igf[:, :2 * H_M].reshape(batch, nc, L, 2 * H_M).transpose(0, 3, 1, 2)
    ig = gates[:, :H_M].reshape(batch * H_M * nc, 1, L)
    lf = gates[:, H_M:].reshape(batch * H_M * nc, 1, L)
    tile = pl.BlockSpec((L, DH_M), lambda b, h, c: (b * nc + c, h))
    gate = pl.BlockSpec((1, 1, L), lambda b, h, c: ((b * H_M + h) * nc + c, 0, 0))
    state = lambda r: pl.BlockSpec((None, None, r, DH_M), lambda b, h, c: (b, h, 0, 0))
    return pl.pallas_call(
        _mlstm_kernel,
        grid=(batch, H_M, nc),
        in_specs=[
            tile,
            pl.BlockSpec((None, DH_M, L), lambda b, h, c: (b, h, c)),
            tile, tile, gate, gate, tile,
            pl.BlockSpec((1, DH_M), lambda b, h, c: (0, h)),
        ],
        out_specs=[tile, state(DH_M), state(1), state(1)],
        out_shape=[
            jax.ShapeDtypeStruct((batch * seq, D_M), BF16),
            jax.ShapeDtypeStruct((batch, H_M, DH_M, DH_M), F32),
            jax.ShapeDtypeStruct((batch, H_M, 1, DH_M), F32),
            jax.ShapeDtypeStruct((batch, H_M, 1, DH_M), F32),
        ],
        scratch_shapes=[pltpu.VMEM((DH_M, DH_M), F32), pltpu.VMEM((1, DH_M), F32), pltpu.VMEM((1, 1), F32)],
        compiler_params=_cp(("parallel", "parallel", "arbitrary")),
        name="mlstm_prompt",
    )(qm, kt, km, vm, ig, lf, gm, mg)


def _conv_epilogue(c, lg_ref, lb_ref, szc_ref, o_ref):
    y = _layernorm_rows(c) * lg_ref[...] + lb_ref[...]
    o_ref[...] = (_silu(y) * szc_ref[...].astype(F32)).astype(o_ref.dtype)


def _conv_kernel(u_ref, halo_ref, w_ref, cb_ref, lg_ref, lb_ref, szc_ref, o_ref, full):
    i = pl.program_id(1)
    tm = u_ref.shape[0]
    full[0:HALO, :] = jnp.where(i > 0, halo_ref[...], 0.0)
    full[HALO:HALO + tm, :] = u_ref[...]
    first = HALO - (CONV_W - 1)
    acc = jnp.zeros((tm, D_C), F32)
    for j in range(CONV_W):
        acc = acc + w_ref[j:j + 1, :] * full[first + j:first + j + tm, :]
    _conv_epilogue(acc + cb_ref[...], lg_ref, lb_ref, szc_ref, o_ref)


def _conv_call(u, wpad, cb, lg, lb, szc, *, batch, seq, tm):
    assert seq % tm == 0 and tm % HALO == 0
    nt = seq // tm
    row = lambda b, i: (b * nt + i, 0)
    const = lambda b, i: (0, 0)
    halo_idx = lambda b, i: (jnp.maximum((b * nt + i) * (tm // HALO) - 1, 0), 0)
    return pl.pallas_call(
        _conv_kernel,
        grid=(batch, nt),
        in_specs=[
            pl.BlockSpec((tm, D_C), row),
            pl.BlockSpec((HALO, D_C), halo_idx),
            pl.BlockSpec((HALO, D_C), const),
            pl.BlockSpec((1, D_C), const),
            pl.BlockSpec((1, D_C), const),
            pl.BlockSpec((1, D_C), const),
            pl.BlockSpec((tm, D_C), row),
        ],
        out_specs=pl.BlockSpec((tm, D_C), row),
        out_shape=jax.ShapeDtypeStruct((batch * seq, D_C), BF16),
        scratch_shapes=[pltpu.VMEM((HALO + tm, D_C), F32)],
        compiler_params=_cp(("parallel", "parallel")),
        name="conv_prompt",
    )(u, u, wpad, cb, lg, lb, szc)


def _attn_sample_kernel(pt_ref, q_ref, kn_ref, vn_ref, sza_ref, *refs, n_pages):
    del pt_ref
    k_refs = refs[:n_pages]
    v_refs = refs[n_pages:2 * n_pages]
    o_ref = refs[2 * n_pages]
    n_blk = n_pages * PAGE_SIZE // MOBA_BLOCK
    ppb = MOBA_BLOCK // PAGE_SIZE
    q = q_ref[0]
    lane = lax.broadcasted_iota(jnp.int32, (H_A, D_A), 1)
    sub = lax.broadcasted_iota(jnp.int32, (H_A, D_A), 0)
    hmask = (lane // DH_A) == sub
    qh = jnp.where(hmask, jnp.broadcast_to(q, (H_A, D_A)), 0.0)
    qhb = qh.astype(BF16)

    scores = []
    for pg in range(n_pages):
        kt = k_refs[pg][...].reshape(D_A, PAGE_SIZE)
        scores.append(_dot(qhb, kt.astype(BF16)))

    gs = []
    for j in range(n_blk):
        tot = jnp.sum(scores[j * ppb], axis=-1, keepdims=True)
        for r in range(1, ppb):
            tot = tot + jnp.sum(scores[j * ppb + r], axis=-1, keepdims=True)
        gs.append(tot * (1.0 / MOBA_BLOCK))

    picked = [jnp.zeros((H_A, 1), F32) for _ in range(n_blk)]
    for _ in range(min(MOBA_TOPK, n_blk)):
        mx = gs[0]
        for j in range(1, n_blk):
            mx = jnp.maximum(mx, gs[j])
        found = jnp.zeros((H_A, 1), F32)
        for j in range(n_blk):
            hit = jnp.where((gs[j] == mx) & (found == 0.0), 1.0, 0.0)
            picked[j] = jnp.maximum(picked[j], hit)
            found = jnp.maximum(found, hit)
            gs[j] = jnp.where(hit > 0.0, -jnp.inf, gs[j])

    s_own = jnp.sum(qh * kn_ref[0], axis=-1, keepdims=True)
    m = s_own
    for pg in range(n_pages):
        scores[pg] = scores[pg] + jnp.where(picked[pg // ppb] > 0.0, 0.0, NEG)
        m = jnp.maximum(m, jnp.max(scores[pg], axis=-1, keepdims=True))
    p_own = jnp.exp2(s_own - m)
    l = p_own
    o = p_own * vn_ref[0]
    for pg in range(n_pages):
        p = jnp.exp2(scores[pg] - m)
        l = l + jnp.sum(p, axis=-1, keepdims=True)
        vt = v_refs[pg][...].reshape(D_A, PAGE_SIZE)
        o = o + _dot_nt(p.astype(BF16), vt.astype(BF16))
    o = o / l
    orow = jnp.sum(jnp.where(hmask, o, 0.0), axis=0, keepdims=True)
    o_ref[0] = orow * sza_ref[0]


def _attn_sample_call(page_table, q, kn, vn, sza, cache_kt, cache_vt, layer):
    db, n_pages = page_table.shape
    assert MOBA_BLOCK % PAGE_SIZE == 0 and (n_pages * PAGE_SIZE) % MOBA_BLOCK == 0
    assert cache_kt.shape[2:] == (H_A, DH_A, PAGE_SIZE)
    rowspec = pl.BlockSpec((1, 1, D_A), lambda b, pt: (b, 0, 0))

    def page(pg):
        return pl.BlockSpec((None, None, H_A, DH_A, PAGE_SIZE), lambda b, pt: (pt[b, pg], layer, 0, 0, 0))

    grid_spec = pltpu.PrefetchScalarGridSpec(
        num_scalar_prefetch=1,
        grid=(db,),
        in_specs=[rowspec] * 4 + [page(pg) for pg in range(n_pages)] * 2,
        out_specs=rowspec,
    )
    r3 = lambda t: t.reshape(db, 1, D_A)
    return pl.pallas_call(
        functools.partial(_attn_sample_kernel, n_pages=n_pages),
        grid_spec=grid_spec,
        out_shape=jax.ShapeDtypeStruct((db, 1, D_A), F32),
        compiler_params=_cp(("parallel",)),
        name="moba_sample",
    )(page_table, r3(q), r3(kn), r3(vn), r3(sza), *([cache_kt] * n_pages), *([cache_vt] * n_pages)).reshape(db, D_A)


def _mlstm_sample_kernel(q_ref, k_ref, v_ref, if_ref, gm_ref, mg_ref, c_ref, n_ref, m_ref,
                         h_ref, c_out, n_out, m_out):
    bb = q_ref.shape[0]

    def body(bi, carry):
        for h in range(H_M):
            sl = slice(h * DH_M, (h + 1) * DH_M)
            q = q_ref[bi, :, sl]
            k = k_ref[bi, :, sl]
            v = v_ref[bi, :, sl]
            ig = if_ref[bi, :, h:h + 1]
            lf = if_ref[bi, :, H_M + h:H_M + h + 1]
            m0 = m_ref[bi, :, h:h + 1]
            c0 = c_ref[bi, h]
            n0 = n_ref[bi, h:h + 1, :]
            q_cols = jnp.broadcast_to(q, (DH_M, DH_M)).T
            k_cols = jnp.broadcast_to(k, (DH_M, DH_M)).T
            inter = lf + m0
            mt = jnp.maximum(inter, ig)
            w = jnp.exp(ig - mt)
            dec = jnp.exp(inter - mt)
            s = jnp.sum(q * k, axis=-1, keepdims=True) * w
            qc = jnp.sum(q_cols * c0, axis=0, keepdims=True)
            num = s * v + dec * qc
            den = s + dec * jnp.sum(q * n0, axis=-1, keepdims=True)
            hm = num / jnp.maximum(jnp.abs(den), jnp.exp(-mt))
            c_out[bi, h] = dec * c0 + w * (k_cols * v)
            n_out[bi, h:h + 1, :] = dec * n0 + w * k
            m_out[bi, :, h:h + 1] = mt
            h_ref[bi, :, sl] = _layernorm_rows(hm) * mg_ref[:, sl] * gm_ref[bi, :, sl]
        return carry

    lax.fori_loop(0, bb, body, 0)


def _mlstm_sample_call(qm, km, vm, igf, gm, mg, c0, n0, m0, layer, *, bb):
    db = qm.shape[0]
    assert db % bb == 0
    depth = m0.shape[0]
    row3 = lambda i: (i, 0, 0)
    r3 = lambda t: t.reshape(db, 1, t.shape[-1])
    tokspec = pl.BlockSpec((bb, 1, D_M), row3)
    h3, c1, n1, m1 = pl.pallas_call(
        _mlstm_sample_kernel,
        grid=(db // bb,),
        in_specs=[
            tokspec, tokspec, tokspec,
            pl.BlockSpec((bb, 1, IF_COLS), row3),
            tokspec,
            pl.BlockSpec((1, D_M), lambda i: (0, 0)),
            pl.BlockSpec((None, bb, H_M, DH_M, DH_M), lambda i: (layer, i, 0, 0, 0)),
            pl.BlockSpec((None, bb, H_M, DH_M), lambda i: (layer, i, 0, 0)),
            pl.BlockSpec((None, bb, 1, H_M), lambda i: (layer, i, 0, 0)),
        ],
        out_specs=[
            tokspec,
            pl.BlockSpec((bb, H_M, DH_M, DH_M), lambda i: (i, 0, 0, 0)),
            pl.BlockSpec((bb, H_M, DH_M), row3),
            pl.BlockSpec((bb, 1, H_M), row3),
        ],
        out_shape=[
            jax.ShapeDtypeStruct((db, 1, D_M), F32),
            jax.ShapeDtypeStruct((db, H_M, DH_M, DH_M), F32),
            jax.ShapeDtypeStruct((db, H_M, DH_M), F32),
            jax.ShapeDtypeStruct((db, 1, H_M), F32),
        ],
        compiler_params=_cp(("parallel",)),
        name="mlstm_sample",
    )(r3(qm), r3(km), r3(vm), r3(igf), r3(gm), mg, c0, n0, m0.reshape(depth, db, 1, H_M))
    return h3.reshape(db, D_M), c1, n1, m1.reshape(db, H_M)


def _conv_sample_kernel(st_ref, u_ref, w_ref, cb_ref, lg_ref, lb_ref, szc_ref, o_ref):
    acc = w_ref[CONV_W - 1:CONV_W, :] * u_ref[...]
    for j in range(CONV_W - 1):
        acc = acc + w_ref[j:j + 1, :] * st_ref[j]
    _conv_epilogue(acc + cb_ref[...], lg_ref, lb_ref, szc_ref, o_ref)


def _conv_sample_call(state_t, u, wpad, cb, lg, lb, szc, *, bb):
    db = u.shape[0]
    row = lambda i: (i, 0)
    const = lambda i: (0, 0)
    return pl.pallas_call(
        _conv_sample_kernel,
        grid=(db // bb,),
        in_specs=[
            pl.BlockSpec((CONV_W - 1, bb, D_C), lambda i: (0, i, 0)),
            pl.BlockSpec((bb, D_C), row),
            pl.BlockSpec((HALO, D_C), const),
            pl.BlockSpec((1, D_C), const),
            pl.BlockSpec((1, D_C), const),
            pl.BlockSpec((1, D_C), const),
            pl.BlockSpec((bb, D_C), row),
        ],
        out_specs=pl.BlockSpec((bb, D_C), row),
        out_shape=jax.ShapeDtypeStruct((db, D_C), F32),
        compiler_params=_cp(("parallel",)),
        name="conv_sample",
    )(state_t, u, wpad, cb, lg, lb, szc)


def _prep_weights(norm_g, w_in, b_igate, b_fgate, qn_g, kn_g, w_a_out, mh_norm_g, w_m_out,
                  conv_w, conv_b, cln_g, cln_b, w_c_out, w_out):
    depth = w_in.shape[0]
    gate0 = 4 * D_A + 5 * D_M
    w = jnp.concatenate([
        w_in[:, :, :gate0], w_in[:, :, gate0 + 2 * H_M:], w_in[:, :, gate0:gate0 + 2 * H_M],
        jnp.zeros((depth, D_MODEL, IF_COLS - 2 * H_M), w_in.dtype)], axis=-1).astype(BF16)
    assert w.shape[-1] == N_COLS
    ifb = jnp.concatenate([b_igate, b_fgate, jnp.zeros((depth, IF_COLS - 2 * H_M), F32)], axis=-1)
    gbd = jnp.asarray(np.kron(np.eye(H_A), np.ones((DH_A, DH_A))), BF16)
    wpad = jnp.concatenate([conv_w, jnp.zeros((depth, HALO - CONV_W, D_C), conv_w.dtype)], axis=1)
    layers = []
    for l in range(depth):
        layers.append(dict(
            ng=norm_g[l][None], w=w[l], gbd=gbd, ifb=ifb[l][None],
            qg=jnp.tile(qn_g[l], H_A)[None], kg=jnp.tile(kn_g[l], H_A)[None],
            wa=w_a_out[l].astype(BF16), wm=w_m_out[l].astype(BF16), wc=w_c_out[l].astype(BF16),
            wo=w_out[l].astype(BF16), mg=mh_norm_g[l][None], wpad=wpad[l],
            cb=conv_b[l][None], lg=cln_g[l][None], lb=cln_b[l][None]))
    return layers


def _prompt_layer(x2d, p, *, batch, seq):
    (q, kf, kb, vf, vb, sza, qm, km, vm, gm, igf, u, szc, g, kmean) = _proj_call(
        x2d, p["ng"], p["w"], p["gbd"], p["qg"], p["kg"], p["ifb"],
        tm=MOBA_BLOCK, act_dtype=BF16, with_kmean=True)
    a_in = _attn_call(q, kb, vb, kmean, sza, batch=batch, seq=seq)
    m_in, c1, n1, m1 = _mlstm_call(qm, km, vm, igf, gm, p["mg"], batch=batch, seq=seq)
    c_in = _conv_call(u, p["wpad"], p["cb"], p["lg"], p["lb"], szc, batch=batch, seq=seq, tm=MOBA_BLOCK)
    y = _out_call(a_in, m_in, c_in, g, x2d, p["wa"], p["wm"], p["wc"], p["wo"], tm=MOBA_BLOCK)
    conv_state = u.reshape(batch, seq, D_C)[:, seq - (CONV_W - 1):]
    return y, kf, vf, c1, n1[:, :, 0], m1[:, :, 0, 0], conv_state


def _sample_layer(x2d, p, page_table, cache_k, cache_v, st_c, st_n, st_m, st_conv, layer):
    db = x2d.shape[0]
    (q, kf, _, vf, _, sza, qm, km, vm, gm, igf, u, szc, g) = _proj_call(
        x2d, p["ng"], p["w"], p["gbd"], p["qg"], p["kg"], p["ifb"],
        tm=db, act_dtype=F32, with_kmean=False)
    a_in = _attn_sample_call(page_table, q, kf, vf, sza, cache_k, cache_v, layer)
    m_in, c1, n1, m1 = _mlstm_sample_call(qm, km, vm, igf, gm, p["mg"], st_c, st_n, st_m, layer, bb=8)
    conv_hist = st_conv[layer]
    c_in = _conv_sample_call(conv_hist.transpose(1, 0, 2), u, p["wpad"], p["cb"], p["lg"], p["lb"], szc, bb=8)
    y = _out_call(a_in, m_in, c_in, g, x2d, p["wa"], p["wm"], p["wc"], p["wo"], tm=db)
    conv_state = jnp.concatenate([conv_hist[:, 1:], u[:, None, :]], axis=1)
    return y, kf, vf, c1, n1, m1, conv_state


def kernel(x_prompt, x_sample, cache_k, cache_v, state_mlstm_C, state_mlstm_n, state_mlstm_m, state_conv,
           page_table, norm_g, w_in, b_igate, b_fgate, qn_g, kn_g, w_a_out, mh_norm_g, w_m_out,
           conv_w, conv_b, cln_g, cln_b, w_c_out, w_out):
    batch, seq, _ = x_prompt.shape
    db, dseq, _ = x_sample.shape
    assert dseq == 1
    depth = w_in.shape[0]
    layers = _prep_weights(norm_g, w_in, b_igate, b_fgate, qn_g, kn_g, w_a_out, mh_norm_g, w_m_out,
                           conv_w, conv_b, cln_g, cln_b, w_c_out, w_out)
    cache_kt = cache_k.transpose(0, 1, 3, 4, 2)
    cache_vt = cache_v.transpose(0, 1, 3, 4, 2)
    yp = x_prompt.reshape(batch * seq, D_MODEL)
    ys = x_sample.reshape(db, D_MODEL)
    pr, sa = [], []
    for l in range(depth):
        yp, *rest = _prompt_layer(yp, layers[l], batch=batch, seq=seq)
        pr.append(rest)
        ys, *rest = _sample_layer(ys, layers[l], page_table, cache_kt, cache_vt,
                                  state_mlstm_C, state_mlstm_n, state_mlstm_m, state_conv, l)
        sa.append(rest)

    def stack(rows, idx, axis):
        return jnp.stack([r[idx] for r in rows], axis=axis)

    k_prompt = jnp.stack([r[0].reshape(batch, seq, H_A, DH_A) for r in pr], axis=1)
    v_prompt = jnp.stack([r[1].reshape(batch, seq, H_A, DH_A) for r in pr], axis=1)
    k_sample = stack(sa, 0, 1).reshape(db, depth, 1, H_A, DH_A)
    v_sample = stack(sa, 1, 1).reshape(db, depth, 1, H_A, DH_A)
    return (yp.reshape(batch, seq, D_MODEL), ys.reshape(db, 1, D_MODEL),
            k_prompt, v_prompt, stack(pr, 2, 0), stack(pr, 3, 0), stack(pr, 4, 0), stack(pr, 5, 0),
            k_sample, v_sample, stack(sa, 2, 0), stack(sa, 3, 0), stack(sa, 4, 0), stack(sa, 5, 0))
```

```python
import functools

import numpy as np
import jax
import jax.numpy as jnp
from jax import lax
from jax.experimental import pallas as pl
from jax.experimental.pallas import tpu as pltpu

F32 = jnp.float32
BF16 = jnp.bfloat16

D_MODEL = 1024
H_A = 8
DH_A = 64
D_A = H_A * DH_A
MOBA_BLOCK = 256
MOBA_TOPK = 3
ATTN_Q = 256
H_M = 4
DH_M = 128
D_M = H_M * DH_M
MLSTM_CHUNK = 128
D_C = 512
CONV_W = 31
N_BRANCH = 3
PAGE_SIZE = 128
EPS = 1e-6
NEG = -1e30
LOG2E = 1.4426950408889634

SEG = 512
N_SEG = 12 + N_BRANCH * D_MODEL // SEG
IF_COLS = 128
N_SEG1 = (4 * D_A + 5 * D_M) // SEG
HALO = 32
LANES = 128
ATTN_GROUP = 2

_NT = (((1,), (1,)), ((), ()))

_VMEM_LIMIT = 52 * 1024 * 1024


def _cp(sem, vmem=_VMEM_LIMIT):
    return pltpu.CompilerParams(dimension_semantics=sem, vmem_limit_bytes=vmem)


def _sigmoid(x):
    return 1.0 / (1.0 + jnp.exp(-x))


def _silu(x):
    return x * _sigmoid(x)


def _log_sigmoid(x):
    return jnp.minimum(x, 0.0) - jnp.log(1.0 + jnp.exp(-jnp.abs(x)))


def _dot(a, b):
    return jnp.dot(a, b, preferred_element_type=F32)


def _dot_nt(a, b):
    return lax.dot_general(a, b, _NT, preferred_element_type=F32)


def _split2(x):
    hi = x.astype(BF16)
    lo = (x - hi.astype(F32)).astype(BF16)
    return hi, lo


def _split3(x):
    hi = x.astype(BF16)
    r = x - hi.astype(F32)
    mid = r.astype(BF16)
    lo = (r - mid.astype(F32)).astype(BF16)
    return hi, mid, lo


def _layernorm_rows(x):
    mu = jnp.mean(x, axis=-1, keepdims=True)
    xc = x - mu
    var = jnp.mean(xc * xc, axis=-1, keepdims=True)
    return xc * lax.rsqrt(var + EPS)


def _proj_kernel(x_ref, ng_ref, w1_ref, w2_ref, wif_ref, gbd_ref, qg_ref, kg_ref, ifb_ref,
                 q_ref, kf_ref, kb_ref, vf_ref, vb_ref, sza_ref, qm_ref, km_ref, vm_ref,
                 gm_ref, if_ref, u_ref, szc_ref, g_ref, *kmean_refs):
    x = x_ref[...]
    h = x * lax.rsqrt(jnp.mean(x * x, axis=-1, keepdims=True) + EPS) * ng_ref[...]
    hb = h.astype(BF16)

    def seg(i):
        w_ref, i = (w1_ref, i) if i < N_SEG1 else (w2_ref, i - N_SEG1)
        return _dot(hb, w_ref[:, i * SEG:(i + 1) * SEG])

    def headnorm(t, g_row):
        hi, lo = _split2(t * t)
        ms = (_dot(hi, gbd_ref[...]) + _dot(lo, gbd_ref[...])) * (1.0 / DH_A)
        return t * lax.rsqrt(ms + EPS) * g_row

    qn = headnorm(seg(0), qg_ref[...])
    q_ref[...] = (qn * (DH_A ** -0.5 * LOG2E)).astype(q_ref.dtype)
    kn = headnorm(seg(1), kg_ref[...])
    kf_ref[...] = kn
    kb_ref[...] = kn.astype(BF16)
    if kmean_refs:
        kmean_ref = kmean_refs[0]
        for r in range(kmean_ref.shape[0]):
            kmean_ref[r] = jnp.mean(kn[r * MOBA_BLOCK:(r + 1) * MOBA_BLOCK], axis=0, keepdims=True)
    v = seg(2)
    vf_ref[...] = v
    vb_ref[...] = v.astype(BF16)
    sza_ref[...] = _silu(seg(3)).astype(sza_ref.dtype)
    qm_ref[...] = seg(4).astype(qm_ref.dtype)
    km_ref[...] = (seg(5) * (DH_M ** -0.5)).astype(km_ref.dtype)
    vm_ref[...] = seg(6).astype(vm_ref.dtype)
    gm_ref[...] = (_sigmoid(seg(7)) * _silu(seg(8))).astype(gm_ref.dtype)
    u_ref[...] = seg(9) * _sigmoid(seg(10))
    szc_ref[...] = _silu(seg(11)).astype(szc_ref.dtype)
    for c in range(N_BRANCH * D_MODEL // SEG):
        g_ref[:, c * SEG:(c + 1) * SEG] = _sigmoid(seg(12 + c)).astype(g_ref.dtype)
    val = _dot(hb, wif_ref[...]) + ifb_ref[...]
    lane = lax.broadcasted_iota(jnp.int32, val.shape, 1)
    if_ref[...] = jnp.where(lane < H_M, val, _log_sigmoid(val))


def _proj_call(x2d, ng, w1, w2, wif, gbd, qg, kg, ifb, layer, *, tm, act_dtype, with_kmean):
    n_tok = x2d.shape[0]
    assert n_tok % tm == 0
    grid = (n_tok // tm,)
    row = lambda i: (i, 0)
    const = lambda i: (0, 0)
    resident = functools.partial(pl.BlockSpec, index_map=const, pipeline_mode=pl.Buffered(1))

    def layer_weight(w):
        return pl.BlockSpec((None,) + w.shape[1:], lambda i: (layer, 0, 0), pipeline_mode=pl.Buffered(1))

    def tok(width, dtype):
        return pl.BlockSpec((tm, width), row), jax.ShapeDtypeStruct((n_tok, width), dtype)

    outs = [
        tok(D_A, act_dtype),
        tok(D_A, F32), tok(D_A, BF16),
        tok(D_A, F32), tok(D_A, BF16),
        tok(D_A, act_dtype),
        tok(D_M, act_dtype), tok(D_M, act_dtype), tok(D_M, act_dtype),
        tok(D_M, act_dtype),
        tok(IF_COLS, F32),
        tok(D_C, F32),
        tok(D_C, act_dtype),
        tok(N_BRANCH * D_MODEL, act_dtype),
    ]
    if with_kmean:
        assert tm % MOBA_BLOCK == 0
        nb = tm // MOBA_BLOCK
        outs.append((pl.BlockSpec((nb, 1, D_A), lambda i: (i, 0, 0)),
                     jax.ShapeDtypeStruct((n_tok // MOBA_BLOCK, 1, D_A), F32)))
    return pl.pallas_call(
        _proj_kernel,
        grid=grid,
        in_specs=[
            pl.BlockSpec((tm, D_MODEL), row),
            resident((1, D_MODEL)),
            layer_weight(w1), layer_weight(w2), layer_weight(wif),
            resident((D_A, D_A)),
            resident((1, D_A)),
            resident((1, D_A)),
            resident((1, IF_COLS)),
        ],
        out_specs=[o[0] for o in outs],
        out_shape=[o[1] for o in outs],
        compiler_params=_cp(("parallel",)),
        name="proj",
    )(x2d, ng, w1, w2, wif, gbd, qg, kg, ifb)


def _out_kernel(a_ref, m_ref, c_ref, g_ref, x_ref, wa_ref, wm_ref, wc_ref, wo_ref, o_ref):
    ya = _dot(a_ref[...].astype(BF16), wa_ref[...])
    ym = _dot(m_ref[...].astype(BF16), wm_ref[...])
    yc = _dot(c_ref[...].astype(BF16), wc_ref[...])
    merged = (g_ref[:, 0:D_MODEL].astype(F32) * ya
              + g_ref[:, D_MODEL:2 * D_MODEL].astype(F32) * ym
              + g_ref[:, 2 * D_MODEL:3 * D_MODEL].astype(F32) * yc)
    o_ref[...] = x_ref[...] + _dot(merged.astype(BF16), wo_ref[...])


def _out_call(a, m, c, g, x2d, wa, wm, wc, wo, *, tm):
    n_tok = x2d.shape[0]
    row = lambda i: (i, 0)
    const = lambda i: (0, 0)
    resident = functools.partial(pl.BlockSpec, index_map=const, pipeline_mode=pl.Buffered(1))
    return pl.pallas_call(
        _out_kernel,
        grid=(n_tok // tm,),
        in_specs=[
            pl.BlockSpec((tm, D_A), row),
            pl.BlockSpec((tm, D_M), row),
            pl.BlockSpec((tm, D_C), row),
            pl.BlockSpec((tm, N_BRANCH * D_MODEL), row),
            pl.BlockSpec((tm, D_MODEL), row),
            resident((D_A, D_MODEL)),
            resident((D_M, D_MODEL)),
            resident((D_C, D_MODEL)),
            resident((D_MODEL, D_MODEL)),
        ],
        out_specs=pl.BlockSpec((tm, D_MODEL), row),
        out_shape=jax.ShapeDtypeStruct((n_tok, D_MODEL), F32),
        compiler_params=_cp(("parallel",)),
        name="outproj",
    )(a, m, c, g, x2d, wa, wm, wc, wo)


def _attn_kernel(q_ref, k_ref, vt0_ref, vt1_ref, km_ref, sza_ref, o_ref, bias_s, s_s, *, nb):
    vt_refs = (vt0_ref, vt1_ref)
    i = pl.program_id(2)
    own = (i * ATTN_Q) // MOBA_BLOCK
    ownf = own.astype(F32)
    q2 = q_ref[...]
    lane = lax.broadcasted_iota(jnp.int32, (ATTN_Q, LANES), 1)
    km_hi, km_mid, km_lo = _split3(km_ref[...])
    blk = lax.broadcasted_iota(jnp.int32, (nb, ATTN_Q), 0).astype(F32)

    qs = []
    for a in range(2):
        qa = jnp.where((lane // DH_A) == a, q2, jnp.zeros_like(q2))
        gs = _dot_nt(km_hi, qa) + _dot_nt(km_mid, qa) + _dot_nt(km_lo, qa)
        gs = jnp.where(blk < ownf, gs, -jnp.inf)
        bias = jnp.where(blk == ownf, 0.0, NEG)
        for _ in range(MOBA_TOPK):
            mx = jnp.max(gs, axis=0, keepdims=True)
            cand = jnp.where((gs == mx) & (mx > -jnp.inf), blk, float(nb))
            idx = jnp.min(cand, axis=0, keepdims=True)
            pick = blk == idx
            bias = jnp.where(pick, 0.0, bias)
            gs = jnp.where(pick, -jnp.inf, gs)
        for n in range(nb):
            bias_s[a, n] = jnp.broadcast_to(bias[n:n + 1, :], (8, ATTN_Q))
        qs.append(qa)

    def scores(j, a, kj):
        return _dot_nt(kj, qs[a]) + bias_s[a, j][0:1, :]

    def fold8(x, op):
        return op(x.reshape(MOBA_BLOCK // 8, 8, ATTN_Q), axis=0)

    def rows(j):
        return pl.ds(pl.multiple_of(j * MOBA_BLOCK, MOBA_BLOCK), MOBA_BLOCK)

    n_full = own // ATTN_GROUP

    def pass1(g, mx):
        mx = list(mx)
        for r in range(ATTN_GROUP):
            j = g * ATTN_GROUP + r
            kj = k_ref[rows(j), :]
            for a in range(2):
                s = scores(j, a, kj)
                s_s[a, rows(j), :] = s
                mx[a] = jnp.maximum(mx[a], fold8(s, jnp.max))
        return tuple(mx)

    neg8 = jnp.full((8, ATTN_Q), NEG, F32)
    mx = list(lax.fori_loop(0, n_full, pass1, (neg8, neg8)))
    kpos = lax.broadcasted_iota(jnp.int32, (MOBA_BLOCK, ATTN_Q), 0)
    qpos = lax.broadcasted_iota(jnp.int32, (MOBA_BLOCK, ATTN_Q), 1) + i * ATTN_Q
    for r in range(ATTN_GROUP):
        j = n_full * ATTN_GROUP + r
        kj = k_ref[rows(j), :]
        visible = (kpos + j * MOBA_BLOCK) <= qpos
        for a in range(2):
            s = jnp.where(visible, scores(j, a, kj), NEG)
            s_s[a, rows(j), :] = s
            mx[a] = jnp.maximum(mx[a], fold8(s, jnp.max))
    m = [jnp.max(mx[a], axis=0, keepdims=True) for a in range(2)]

    def accumulate(j, acc):
        return [acc[a] + _dot(vt_refs[a][j], jnp.exp2(s_s[a, rows(j), :] - m[a]).astype(BF16))
                for a in range(2)]

    def pass2(g, acc):
        for r in range(ATTN_GROUP):
            acc = accumulate(g * ATTN_GROUP + r, acc)
        return tuple(acc)

    zero = jnp.zeros((LANES, ATTN_Q), F32)
    acc = lax.fori_loop(0, n_full, pass2, (zero, zero))
    for r in range(ATTN_GROUP):
        acc = accumulate(n_full * ATTN_GROUP + r, acc)
    inv0 = 1.0 / acc[0][DH_A:DH_A + 1, :]
    inv1 = 1.0 / acc[1][0:1, :]
    drow = lax.broadcasted_iota(jnp.int32, (LANES, ATTN_Q), 0)
    ot = jnp.where((drow // DH_A) == 0, acc[0] * inv0, acc[1] * inv1)
    o_ref[...] = (ot.T * sza_ref[...].astype(F32)).astype(o_ref.dtype)


def _attn_call(q, kb, vb, kmean, sza, *, batch, seq):
    assert seq % (MOBA_BLOCK * ATTN_GROUP) == 0 and MOBA_BLOCK % ATTN_Q == 0
    nb = seq // MOBA_BLOCK
    nq = seq // ATTN_Q
    n_pair = D_A // LANES
    vt = vb.reshape(batch, nb, MOBA_BLOCK, D_A).transpose(0, 1, 3, 2)
    head_in_pair = (lax.broadcasted_iota(jnp.int32, (1, 1, D_A, 1), 2) // DH_A) % 2
    vts = [jnp.where(head_in_pair == a, vt, jnp.ones_like(vt)) for a in range(2)]
    vt_spec = pl.BlockSpec((None, nb, LANES, MOBA_BLOCK), lambda b, p, i: (b, 0, p, 0))
    return pl.pallas_call(
        functools.partial(_attn_kernel, nb=nb),
        grid=(batch, n_pair, nq),
        in_specs=[
            pl.BlockSpec((ATTN_Q, LANES), lambda b, p, i: (b * nq + i, p)),
            pl.BlockSpec((seq, LANES), lambda b, p, i: (b, p)),
            vt_spec, vt_spec,
            pl.BlockSpec((None, nb, LANES), lambda b, p, i: (b, 0, p)),
            pl.BlockSpec((ATTN_Q, LANES), lambda b, p, i: (b * nq + i, p)),
        ],
        out_specs=pl.BlockSpec((ATTN_Q, LANES), lambda b, p, i: (b * nq + i, p)),
        out_shape=jax.ShapeDtypeStruct((batch * seq, D_A), BF16),
        scratch_shapes=[pltpu.VMEM((2, nb, 8, ATTN_Q), F32), pltpu.VMEM((2, seq, ATTN_Q), F32)],
        compiler_params=_cp(("parallel", "parallel", "parallel")),
        name="moba_prompt",
    )(q, kb, vts[0], vts[1], kmean.reshape(batch, nb, D_A), sza)


def _mlstm_kernel(q_ref, kt_ref, k_ref, v_ref, gate_ref, gm_ref, mg_ref,
                  h_ref, c_out, n_out, m_out, c_s, n_s, m_s):
    c = pl.program_id(0)
    L = MLSTM_CHUNK

    @pl.when(c == 0)
    def _():
        c_s[...] = jnp.zeros_like(c_s)
        n_s[...] = jnp.zeros_like(n_s)
        m_s[...] = jnp.zeros_like(m_s)

    t_i = lax.broadcasted_iota(jnp.int32, (L, L), 0)
    s_i = lax.broadcasted_iota(jnp.int32, (L, L), 1)
    causal = s_i <= t_i

    def chain(b, h):
        sl = slice(h * DH_M, (h + 1) * DH_M)
        q = q_ref[b, :, sl]
        kt = kt_ref[b, sl, :]
        k = k_ref[b, :, sl]
        v = v_ref[b, :, sl]
        ig = gate_ref[b, h:h + 1, :]
        lf = gate_ref[b, H_M + h:H_M + h + 1, :]
        b_col = jnp.sum(jnp.where(causal, jnp.broadcast_to(lf, (L, L)), 0.0), axis=-1, keepdims=True)
        b_t = jnp.broadcast_to(b_col, (L, L))
        b_s = b_t.T
        m_prev = m_s[b, h]
        n_prev = n_s[b, h]
        c_prev = c_s[b, h]
        dmat = jnp.where(causal, b_t - b_s + jnp.broadcast_to(ig, (L, L)), NEG)
        inter = b_col + m_prev
        mt = jnp.maximum(inter, jnp.max(dmat, axis=-1, keepdims=True))
        w = jnp.exp(dmat - mt)
        dec = jnp.exp(inter - mt)
        s = _dot(q, kt) * w
        num = _dot(s.astype(BF16), v) + dec * _dot(q, c_prev.astype(BF16))
        den = jnp.sum(s, axis=-1, keepdims=True) + dec * jnp.sum(q.astype(F32) * n_prev, axis=-1, keepdims=True)
        hm = num / jnp.maximum(jnp.abs(den), jnp.exp(-mt))

        m_end = mt[L - 1:L, :]
        b_last = b_col[L - 1:L, :]
        b_row = b_s[0:1, :]
        dec_end = jnp.exp(b_last + m_prev - m_end)
        ws = jnp.exp(b_last - b_row + ig - m_end)
        kws = (kt.astype(F32) * ws).astype(BF16)
        c_s[b, h] = dec_end * c_prev + _dot(kws, v)
        ws_hi, ws_lo = _split2(jnp.broadcast_to(ws, (8, L)))
        n_s[b, h] = dec_end * n_prev + (_dot(ws_hi, k) + _dot(ws_lo, k))[0:1, :]
        m_s[b, h] = m_end

        hn = _layernorm_rows(hm) * mg_ref[:, sl] * gm_ref[b, :, sl].astype(F32)
        h_ref[b, :, sl] = hn.astype(h_ref.dtype)

    for b in range(q_ref.shape[0]):
        for h in range(H_M):
            chain(b, h)

    @pl.when(c == pl.num_programs(0) - 1)
    def _():
        c_out[...] = c_s[...]
        n_out[...] = n_s[...]
        m_out[...] = jnp.broadcast_to(m_s[...], m_out.shape)


def _mlstm_call(qm, km, vm, igf, gm, mg, *, batch, seq):
    L = MLSTM_CHUNK
    assert seq % L == 0
    nc = seq // L
    r3 = lambda t: t.reshape(batch, seq, D_M)
    kt = r3(km).transpose(0, 2, 1)
    gates = igf[:, :2 * H_M].reshape(batch, nc, L, 2 * H_M).transpose(1, 0, 3, 2)
    tile = pl.BlockSpec((batch, L, D_M), lambda c: (0, c, 0))
    state = lambda r, w: pl.BlockSpec((batch, H_M, r, w), lambda c: (0, 0, 0, 0))
    h3, c1, n1, m1 = pl.pallas_call(
        _mlstm_kernel,
        grid=(nc,),
        in_specs=[
            tile,
            pl.BlockSpec((batch, D_M, L), lambda c: (0, 0, c)),
            tile, tile,
            pl.BlockSpec((None, batch, 2 * H_M, L), lambda c: (c, 0, 0, 0)),
            tile,
            pl.BlockSpec((1, D_M), lambda c: (0, 0)),
        ],
        out_specs=[tile, state(DH_M, DH_M), state(1, DH_M), state(1, DH_M)],
        out_shape=[
            jax.ShapeDtypeStruct((batch, seq, D_M), BF16),
            jax.ShapeDtypeStruct((batch, H_M, DH_M, DH_M), F32),
            jax.ShapeDtypeStruct((batch, H_M, 1, DH_M), F32),
            jax.ShapeDtypeStruct((batch, H_M, 1, DH_M), F32),
        ],
        scratch_shapes=[pltpu.VMEM((batch, H_M, DH_M, DH_M), F32), pltpu.VMEM((batch, H_M, 1, DH_M), F32),
                        pltpu.VMEM((batch, H_M, 1, 1), F32)],
        compiler_params=_cp(("arbitrary",)),
        name="mlstm_prompt",
    )(r3(qm), kt, r3(km), r3(vm), gates, r3(gm), mg)
    return h3.reshape(batch * seq, D_M), c1, n1, m1


def _conv_epilogue(c, lg_ref, lb_ref, szc_ref, o_ref):
    y = _layernorm_rows(c) * lg_ref[...] + lb_ref[...]
    o_ref[...] = (_silu(y) * szc_ref[...].astype(F32)).astype(o_ref.dtype)


def _conv_kernel(u_ref, halo_ref, w_ref, cb_ref, lg_ref, lb_ref, szc_ref, o_ref, full):
    i = pl.program_id(1)
    tm = u_ref.shape[0]
    full[0:HALO, :] = jnp.where(i > 0, halo_ref[...], 0.0)
    full[HALO:HALO + tm, :] = u_ref[...]
    first = HALO - (CONV_W - 1)
    acc = jnp.zeros((tm, D_C), F32)
    for j in range(CONV_W):
        acc = acc + w_ref[j:j + 1, :] * full[first + j:first + j + tm, :]
    _conv_epilogue(acc + cb_ref[...], lg_ref, lb_ref, szc_ref, o_ref)


def _conv_call(u, wpad, cb, lg, lb, szc, *, batch, seq, tm):
    assert seq % tm == 0 and tm % HALO == 0
    nt = seq // tm
    row = lambda b, i: (b * nt + i, 0)
    const = lambda b, i: (0, 0)
    halo_idx = lambda b, i: (jnp.maximum((b * nt + i) * (tm // HALO) - 1, 0), 0)
    return pl.pallas_call(
        _conv_kernel,
        grid=(batch, nt),
        in_specs=[
            pl.BlockSpec((tm, D_C), row),
            pl.BlockSpec((HALO, D_C), halo_idx),
            pl.BlockSpec((HALO, D_C), const),
            pl.BlockSpec((1, D_C), const),
            pl.BlockSpec((1, D_C), const),
            pl.BlockSpec((1, D_C), const),
            pl.BlockSpec((tm, D_C), row),
        ],
        out_specs=pl.BlockSpec((tm, D_C), row),
        out_shape=jax.ShapeDtypeStruct((batch * seq, D_C), BF16),
        scratch_shapes=[pltpu.VMEM((HALO + tm, D_C), F32)],
        compiler_params=_cp(("parallel", "parallel")),
        name="conv_prompt",
    )(u, u, wpad, cb, lg, lb, szc)


def _attn_sample_kernel(pt_ref, q_ref, kn_ref, vn_ref, sza_ref, *refs, n_pages):
    del pt_ref
    k_refs = refs[:n_pages]
    v_refs = refs[n_pages:2 * n_pages]
    o_ref = refs[2 * n_pages]
    n_blk = n_pages * PAGE_SIZE // MOBA_BLOCK
    ppb = MOBA_BLOCK // PAGE_SIZE
    q = q_ref[0]
    lane = lax.broadcasted_iota(jnp.int32, (H_A, D_A), 1)
    sub = lax.broadcasted_iota(jnp.int32, (H_A, D_A), 0)
    hmask = (lane // DH_A) == sub
    qh = jnp.where(hmask, jnp.broadcast_to(q, (H_A, D_A)), 0.0)
    qhb = qh.astype(BF16)

    scores = []
    for pg in range(n_pages):
        kt = k_refs[pg][...].reshape(D_A, PAGE_SIZE)
        scores.append(_dot(qhb, kt.astype(BF16)))

    gs = []
    for j in range(n_blk):
        tot = jnp.sum(scores[j * ppb], axis=-1, keepdims=True)
        for r in range(1, ppb):
            tot = tot + jnp.sum(scores[j * ppb + r], axis=-1, keepdims=True)
        gs.append(tot * (1.0 / MOBA_BLOCK))

    picked = [jnp.zeros((H_A, 1), F32) for _ in range(n_blk)]
    for _ in range(min(MOBA_TOPK, n_blk)):
        mx = gs[0]
        for j in range(1, n_blk):
            mx = jnp.maximum(mx, gs[j])
        found = jnp.zeros((H_A, 1), F32)
        for j in range(n_blk):
            hit = jnp.where((gs[j] == mx) & (found == 0.0), 1.0, 0.0)
            picked[j] = jnp.maximum(picked[j], hit)
            found = jnp.maximum(found, hit)
            gs[j] = jnp.where(hit > 0.0, -jnp.inf, gs[j])

    s_own = jnp.sum(qh * kn_ref[0], axis=-1, keepdims=True)
    m = s_own
    for pg in range(n_pages):
        scores[pg] = scores[pg] + jnp.where(picked[pg // ppb] > 0.0, 0.0, NEG)
        m = jnp.maximum(m, jnp.max(scores[pg], axis=-1, keepdims=True))
    p_own = jnp.exp2(s_own - m)
    l = p_own
    o = p_own * vn_ref[0]
    for pg in range(n_pages):
        p = jnp.exp2(scores[pg] - m)
        l = l + jnp.sum(p, axis=-1, keepdims=True)
        vt = v_refs[pg][...].reshape(D_A, PAGE_SIZE)
        o = o + _dot_nt(p.astype(BF16), vt.astype(BF16))
    o = o / l
    orow = jnp.sum(jnp.where(hmask, o, 0.0), axis=0, keepdims=True)
    o_ref[0] = orow * sza_ref[0]


def _attn_sample_call(page_table, q, kn, vn, sza, cache_kt, cache_vt, layer):
    db, n_pages = page_table.shape
    assert MOBA_BLOCK % PAGE_SIZE == 0 and (n_pages * PAGE_SIZE) % MOBA_BLOCK == 0
    assert cache_kt.shape[2:] == (H_A, DH_A, PAGE_SIZE)
    rowspec = pl.BlockSpec((1, 1, D_A), lambda b, pt: (b, 0, 0))

    def page(pg):
        return pl.BlockSpec((None, None, H_A, DH_A, PAGE_SIZE), lambda b, pt: (pt[b, pg], layer, 0, 0, 0))

    grid_spec = pltpu.PrefetchScalarGridSpec(
        num_scalar_prefetch=1,
        grid=(db,),
        in_specs=[rowspec] * 4 + [page(pg) for pg in range(n_pages)] * 2,
        out_specs=rowspec,
    )
    r3 = lambda t: t.reshape(db, 1, D_A)
    return pl.pallas_call(
        functools.partial(_attn_sample_kernel, n_pages=n_pages),
        grid_spec=grid_spec,
        out_shape=jax.ShapeDtypeStruct((db, 1, D_A), F32),
        compiler_params=_cp(("parallel",)),
        name="moba_sample",
    )(page_table, r3(q), r3(kn), r3(vn), r3(sza), *([cache_kt] * n_pages), *([cache_vt] * n_pages)).reshape(db, D_A)


def _mlstm_sample_kernel(q_ref, k_ref, v_ref, if_ref, gm_ref, mg_ref, c_ref, n_ref, m_ref,
                         h_ref, c_out, n_out, m_out):
    bb = q_ref.shape[0]

    def body(bi, carry):
        for h in range(H_M):
            sl = slice(h * DH_M, (h + 1) * DH_M)
            q = q_ref[bi, :, sl]
            k = k_ref[bi, :, sl]
            v = v_ref[bi, :, sl]
            ig = if_ref[bi, :, h:h + 1]
            lf = if_ref[bi, :, H_M + h:H_M + h + 1]
            m0 = m_ref[bi, :, h:h + 1]
            c0 = c_ref[bi, h]
            n0 = n_ref[bi, h:h + 1, :]
            q_cols = jnp.broadcast_to(q, (DH_M, DH_M)).T
            k_cols = jnp.broadcast_to(k, (DH_M, DH_M)).T
            inter = lf + m0
            mt = jnp.maximum(inter, ig)
            w = jnp.exp(ig - mt)
            dec = jnp.exp(inter - mt)
            s = jnp.sum(q * k, axis=-1, keepdims=True) * w
            qc = jnp.sum(q_cols * c0, axis=0, keepdims=True)
            num = s * v + dec * qc
            den = s + dec * jnp.sum(q * n0, axis=-1, keepdims=True)
            hm = num / jnp.maximum(jnp.abs(den), jnp.exp(-mt))
            c_out[bi, h] = dec * c0 + w * (k_cols * v)
            n_out[bi, h:h + 1, :] = dec * n0 + w * k
            m_out[bi, :, h:h + 1] = mt
            h_ref[bi, :, sl] = _layernorm_rows(hm) * mg_ref[:, sl] * gm_ref[bi, :, sl]
        return carry

    lax.fori_loop(0, bb, body, 0)


def _mlstm_sample_call(qm, km, vm, igf, gm, mg, c0, n0, m0, layer, *, bb):
    db = qm.shape[0]
    assert db % bb == 0
    depth = m0.shape[0]
    row3 = lambda i: (i, 0, 0)
    r3 = lambda t: t.reshape(db, 1, t.shape[-1])
    tokspec = pl.BlockSpec((bb, 1, D_M), row3)
    h3, c1, n1, m1 = pl.pallas_call(
        _mlstm_sample_kernel,
        grid=(db // bb,),
        in_specs=[
            tokspec, tokspec, tokspec,
            pl.BlockSpec((bb, 1, IF_COLS), row3),
            tokspec,
            pl.BlockSpec((1, D_M), lambda i: (0, 0)),
            pl.BlockSpec((None, bb, H_M, DH_M, DH_M), lambda i: (layer, i, 0, 0, 0)),
            pl.BlockSpec((None, bb, H_M, DH_M), lambda i: (layer, i, 0, 0)),
            pl.BlockSpec((None, bb, 1, H_M), lambda i: (layer, i, 0, 0)),
        ],
        out_specs=[
            tokspec,
            pl.BlockSpec((bb, H_M, DH_M, DH_M), lambda i: (i, 0, 0, 0)),
            pl.BlockSpec((bb, H_M, DH_M), row3),
            pl.BlockSpec((bb, 1, H_M), row3),
        ],
        out_shape=[
            jax.ShapeDtypeStruct((db, 1, D_M), F32),
            jax.ShapeDtypeStruct((db, H_M, DH_M, DH_M), F32),
            jax.ShapeDtypeStruct((db, H_M, DH_M), F32),
            jax.ShapeDtypeStruct((db, 1, H_M), F32),
        ],
        compiler_params=_cp(("parallel",)),
        name="mlstm_sample",
    )(r3(qm), r3(km), r3(vm), r3(igf), r3(gm), mg, c0, n0, m0.reshape(depth, db, 1, H_M))
    return h3.reshape(db, D_M), c1, n1, m1.reshape(db, H_M)


def _conv_sample_kernel(st_ref, u_ref, w_ref, cb_ref, lg_ref, lb_ref, szc_ref, o_ref):
    acc = w_ref[CONV_W - 1:CONV_W, :] * u_ref[...]
    for j in range(CONV_W - 1):
        acc = acc + w_ref[j:j + 1, :] * st_ref[j]
    _conv_epilogue(acc + cb_ref[...], lg_ref, lb_ref, szc_ref, o_ref)


def _conv_sample_call(state_t, u, wpad, cb, lg, lb, szc, *, bb):
    db = u.shape[0]
    row = lambda i: (i, 0)
    const = lambda i: (0, 0)
    return pl.pallas_call(
        _conv_sample_kernel,
        grid=(db // bb,),
        in_specs=[
            pl.BlockSpec((CONV_W - 1, bb, D_C), lambda i: (0, i, 0)),
            pl.BlockSpec((bb, D_C), row),
            pl.BlockSpec((HALO, D_C), const),
            pl.BlockSpec((1, D_C), const),
            pl.BlockSpec((1, D_C), const),
            pl.BlockSpec((1, D_C), const),
            pl.BlockSpec((bb, D_C), row),
        ],
        out_specs=pl.BlockSpec((bb, D_C), row),
        out_shape=jax.ShapeDtypeStruct((db, D_C), F32),
        compiler_params=_cp(("parallel",)),
        name="conv_sample",
    )(state_t, u, wpad, cb, lg, lb, szc)


def _prep_weights(norm_g, w_in, b_igate, b_fgate, qn_g, kn_g, w_a_out, mh_norm_g, w_m_out,
                  conv_w, conv_b, cln_g, cln_b, w_c_out, w_out):
    depth = w_in.shape[0]
    gate0 = 4 * D_A + 5 * D_M
    w1 = w_in[:, :, :gate0].astype(BF16)
    w2 = w_in[:, :, gate0 + 2 * H_M:].astype(BF16)
    assert w1.shape[-1] == N_SEG1 * SEG and w2.shape[-1] == (N_SEG - N_SEG1) * SEG
    wif = jnp.concatenate([w_in[:, :, gate0:gate0 + 2 * H_M],
                           jnp.zeros((depth, D_MODEL, IF_COLS - 2 * H_M), w_in.dtype)], axis=-1).astype(BF16)
    ifb = jnp.concatenate([b_igate, b_fgate, jnp.zeros((depth, IF_COLS - 2 * H_M), F32)], axis=-1)
    gbd = jnp.asarray(np.kron(np.eye(H_A), np.ones((DH_A, DH_A))), BF16)
    wpad = jnp.concatenate([conv_w, jnp.zeros((depth, HALO - CONV_W, D_C), conv_w.dtype)], axis=1)
    layers = []
    for l in range(depth):
        layers.append(dict(
            ng=norm_g[l][None], w1=w1, w2=w2, wif=wif, layer=l, gbd=gbd, ifb=ifb[l][None],
            qg=jnp.tile(qn_g[l], H_A)[None], kg=jnp.tile(kn_g[l], H_A)[None],
            wa=w_a_out[l].astype(BF16), wm=w_m_out[l].astype(BF16), wc=w_c_out[l].astype(BF16),
            wo=w_out[l].astype(BF16), mg=mh_norm_g[l][None], wpad=wpad[l],
            cb=conv_b[l][None], lg=cln_g[l][None], lb=cln_b[l][None]))
    return layers


def _prompt_layer(x2d, p, *, batch, seq):
    (q, kf, kb, vf, vb, sza, qm, km, vm, gm, igf, u, szc, g, kmean) = _proj_call(
        x2d, p["ng"], p["w1"], p["w2"], p["wif"], p["gbd"], p["qg"], p["kg"], p["ifb"], p["layer"],
        tm=MOBA_BLOCK, act_dtype=BF16, with_kmean=True)
    a_in = _attn_call(q, kb, vb, kmean, sza, batch=batch, seq=seq)
    m_in, c1, n1, m1 = _mlstm_call(qm, km, vm, igf, gm, p["mg"], batch=batch, seq=seq)
    c_in = _conv_call(u, p["wpad"], p["cb"], p["lg"], p["lb"], szc, batch=batch, seq=seq, tm=MOBA_BLOCK)
    y = _out_call(a_in, m_in, c_in, g, x2d, p["wa"], p["wm"], p["wc"], p["wo"], tm=MOBA_BLOCK)
    conv_state = u.reshape(batch, seq, D_C)[:, seq - (CONV_W - 1):]
    return y, kf, vf, c1, n1[:, :, 0], m1[:, :, 0, 0], conv_state


def _sample_layer(x2d, p, page_table, cache_k, cache_v, st_c, st_n, st_m, st_conv, layer):
    db = x2d.shape[0]
    (q, kf, _, vf, _, sza, qm, km, vm, gm, igf, u, szc, g) = _proj_call(
        x2d, p["ng"], p["w1"], p["w2"], p["wif"], p["gbd"], p["qg"], p["kg"], p["ifb"], p["layer"],
        tm=db, act_dtype=F32, with_kmean=False)
    a_in = _attn_sample_call(page_table, q, kf, vf, sza, cache_k, cache_v, layer)
    m_in, c1, n1, m1 = _mlstm_sample_call(qm, km, vm, igf, gm, p["mg"], st_c, st_n, st_m, layer, bb=8)
    conv_hist = st_conv[layer]
    c_in = _conv_sample_call(conv_hist.transpose(1, 0, 2), u, p["wpad"], p["cb"], p["lg"], p["lb"], szc, bb=8)
    y = _out_call(a_in, m_in, c_in, g, x2d, p["wa"], p["wm"], p["wc"], p["wo"], tm=db)
    conv_state = jnp.concatenate([conv_hist[:, 1:], u[:, None, :]], axis=1)
    return y, kf, vf, c1, n1, m1, conv_state


def kernel(x_prompt, x_sample, cache_k, cache_v, state_mlstm_C, state_mlstm_n, state_mlstm_m, state_conv,
           page_table, norm_g, w_in, b_igate, b_fgate, qn_g, kn_g, w_a_out, mh_norm_g, w_m_out,
           conv_w, conv_b, cln_g, cln_b, w_c_out, w_out):
    batch, seq, _ = x_prompt.shape
    db, dseq, _ = x_sample.shape
    assert dseq == 1
    depth = w_in.shape[0]
    layers = _prep_weights(norm_g, w_in, b_igate, b_fgate, qn_g, kn_g, w_a_out, mh_norm_g, w_m_out,
                           conv_w, conv_b, cln_g, cln_b, w_c_out, w_out)
    cache_kt = cache_k.transpose(0, 1, 3, 4, 2)
    cache_vt = cache_v.transpose(0, 1, 3, 4, 2)
    yp = x_prompt.reshape(batch * seq, D_MODEL)
    ys = x_sample.reshape(db, D_MODEL)
    pr, sa = [], []
    for l in range(depth):
        yp, *rest = _prompt_layer(yp, layers[l], batch=batch, seq=seq)
        pr.append(rest)
        ys, *rest = _sample_layer(ys, layers[l], page_table, cache_kt, cache_vt,
                                  state_mlstm_C, state_mlstm_n, state_mlstm_m, state_conv, l)
        sa.append(rest)

    def stack(rows, idx, axis):
        return jnp.stack([r[idx] for r in rows], axis=axis)

    k_prompt = jnp.stack([r[0].reshape(batch, seq, H_A, DH_A) for r in pr], axis=1)
    v_prompt = jnp.stack([r[1].reshape(batch, seq, H_A, DH_A) for r in pr], axis=1)
    k_sample = stack(sa, 0, 1).reshape(db, depth, 1, H_A, DH_A)
    v_sample = stack(sa, 1, 1).reshape(db, depth, 1, H_A, DH_A)
    return (yp.reshape(batch, seq, D_MODEL), ys.reshape(db, 1, D_MODEL),
            k_prompt, v_prompt, stack(pr, 2, 0), stack(pr, 3, 0), stack(pr, 4, 0), stack(pr, 5, 0),
            k_sample, v_sample, stack(sa, 2, 0), stack(sa, 3, 0), stack(sa, 4, 0), stack(sa, 5, 0))
```

```python
import functools

import numpy as np
import jax
import jax.numpy as jnp
from jax import lax
from jax.experimental import pallas as pl
from jax.experimental.pallas import tpu as pltpu

F32 = jnp.float32
BF16 = jnp.bfloat16

D_MODEL = 1024
H_A = 8
DH_A = 64
D_A = H_A * DH_A
MOBA_BLOCK = 256
MOBA_TOPK = 3
ATTN_Q = 256
H_M = 4
DH_M = 128
D_M = H_M * DH_M
MLSTM_CHUNK = 128
D_C = 512
CONV_W = 31
N_BRANCH = 3
PAGE_SIZE = 128
EPS = 1e-6
NEG = -1e30
LOG2E = 1.4426950408889634

SEG = 512
N_SEG = 12 + N_BRANCH * D_MODEL // SEG
IF_COLS = 128
N_SEG1 = (4 * D_A + 5 * D_M) // SEG
HALO = 32
LANES = 128
ATTN_GROUP = 4

_NT = (((1,), (1,)), ((), ()))

_VMEM_LIMIT = 52 * 1024 * 1024


def _cp(sem, vmem=_VMEM_LIMIT):
    return pltpu.CompilerParams(dimension_semantics=sem, vmem_limit_bytes=vmem)


def _sigmoid(x):
    return 1.0 / (1.0 + jnp.exp(-x))


def _silu(x):
    return x * _sigmoid(x)


def _log_sigmoid(x):
    return jnp.minimum(x, 0.0) - jnp.log(1.0 + jnp.exp(-jnp.abs(x)))


def _dot(a, b):
    return jnp.dot(a, b, preferred_element_type=F32)


def _dot_nt(a, b):
    return lax.dot_general(a, b, _NT, preferred_element_type=F32)


def _split2(x):
    hi = x.astype(BF16)
    lo = (x - hi.astype(F32)).astype(BF16)
    return hi, lo


def _split3(x):
    hi = x.astype(BF16)
    r = x - hi.astype(F32)
    mid = r.astype(BF16)
    lo = (r - mid.astype(F32)).astype(BF16)
    return hi, mid, lo


def _layernorm_rows(x):
    mu = jnp.mean(x, axis=-1, keepdims=True)
    xc = x - mu
    var = jnp.mean(xc * xc, axis=-1, keepdims=True)
    return xc * lax.rsqrt(var + EPS)


def _proj_kernel(x_ref, ng_ref, w1_ref, w2_ref, wif_ref, gbd_ref, qg_ref, kg_ref, ifb_ref,
                 q_ref, kf_ref, kb_ref, vf_ref, vb_ref, sza_ref, qm_ref, km_ref, vm_ref,
                 gm_ref, if_ref, u_ref, szc_ref, g_ref, *kmean_refs):
    x = x_ref[...]
    h = x * lax.rsqrt(jnp.mean(x * x, axis=-1, keepdims=True) + EPS) * ng_ref[...]
    hb = h.astype(BF16)

    def seg(i):
        w_ref, i = (w1_ref, i) if i < N_SEG1 else (w2_ref, i - N_SEG1)
        return _dot(hb, w_ref[:, i * SEG:(i + 1) * SEG])

    def headnorm(t, g_row):
        hi, lo = _split2(t * t)
        ms = (_dot(hi, gbd_ref[...]) + _dot(lo, gbd_ref[...])) * (1.0 / DH_A)
        return t * lax.rsqrt(ms + EPS) * g_row

    qn = headnorm(seg(0), qg_ref[...])
    q_ref[...] = (qn * (DH_A ** -0.5 * LOG2E)).astype(q_ref.dtype)
    kn = headnorm(seg(1), kg_ref[...])
    kf_ref[...] = kn
    kb_ref[...] = kn.astype(BF16)
    if kmean_refs:
        kmean_ref = kmean_refs[0]
        for r in range(kmean_ref.shape[0]):
            kmean_ref[r] = jnp.mean(kn[r * MOBA_BLOCK:(r + 1) * MOBA_BLOCK], axis=0, keepdims=True)
    v = seg(2)
    vf_ref[...] = v
    vb_ref[...] = v.astype(BF16)
    sza_ref[...] = _silu(seg(3)).astype(sza_ref.dtype)
    qm_ref[...] = seg(4).astype(qm_ref.dtype)
    km_ref[...] = (seg(5) * (DH_M ** -0.5)).astype(km_ref.dtype)
    vm_ref[...] = seg(6).astype(vm_ref.dtype)
    gm_ref[...] = (_sigmoid(seg(7)) * _silu(seg(8))).astype(gm_ref.dtype)
    u_ref[...] = seg(9) * _sigmoid(seg(10))
    szc_ref[...] = _silu(seg(11)).astype(szc_ref.dtype)
    for c in range(N_BRANCH * D_MODEL // SEG):
        g_ref[:, c * SEG:(c + 1) * SEG] = _sigmoid(seg(12 + c)).astype(g_ref.dtype)
    val = _dot(hb, wif_ref[...]) + ifb_ref[...]
    lane = lax.broadcasted_iota(jnp.int32, val.shape, 1)
    if_ref[...] = jnp.where(lane < H_M, val, _log_sigmoid(val))


def _proj_call(x2d, ng, w1, w2, wif, gbd, qg, kg, ifb, layer, *, tm, act_dtype, with_kmean):
    n_tok = x2d.shape[0]
    assert n_tok % tm == 0
    grid = (n_tok // tm,)
    row = lambda i: (i, 0)
    const = lambda i: (0, 0)
    resident = functools.partial(pl.BlockSpec, index_map=const, pipeline_mode=pl.Buffered(1))

    def layer_weight(w):
        return pl.BlockSpec((None,) + w.shape[1:], lambda i: (layer, 0, 0), pipeline_mode=pl.Buffered(1))

    def tok(width, dtype):
        return pl.BlockSpec((tm, width), row), jax.ShapeDtypeStruct((n_tok, width), dtype)

    outs = [
        tok(D_A, act_dtype),
        tok(D_A, F32), tok(D_A, BF16),
        tok(D_A, F32), tok(D_A, BF16),
        tok(D_A, act_dtype),
        tok(D_M, act_dtype), tok(D_M, act_dtype), tok(D_M, act_dtype),
        tok(D_M, act_dtype),
        tok(IF_COLS, F32),
        tok(D_C, F32),
        tok(D_C, act_dtype),
        tok(N_BRANCH * D_MODEL, act_dtype),
    ]
    if with_kmean:
        assert tm % MOBA_BLOCK == 0
        nb = tm // MOBA_BLOCK
        outs.append((pl.BlockSpec((nb, 1, D_A), lambda i: (i, 0, 0)),
                     jax.ShapeDtypeStruct((n_tok // MOBA_BLOCK, 1, D_A), F32)))
    return pl.pallas_call(
        _proj_kernel,
        grid=grid,
        in_specs=[
            pl.BlockSpec((tm, D_MODEL), row),
            resident((1, D_MODEL)),
            layer_weight(w1), layer_weight(w2), layer_weight(wif),
            resident((D_A, D_A)),
            resident((1, D_A)),
            resident((1, D_A)),
            resident((1, IF_COLS)),
        ],
        out_specs=[o[0] for o in outs],
        out_shape=[o[1] for o in outs],
        compiler_params=_cp(("parallel",)),
        name="proj",
    )(x2d, ng, w1, w2, wif, gbd, qg, kg, ifb)


def _out_kernel(a_ref, m_ref, c_ref, g_ref, x_ref, wa_ref, wm_ref, wc_ref, wo_ref, o_ref):
    ya = _dot(a_ref[...].astype(BF16), wa_ref[...])
    ym = _dot(m_ref[...].astype(BF16), wm_ref[...])
    yc = _dot(c_ref[...].astype(BF16), wc_ref[...])
    merged = (g_ref[:, 0:D_MODEL].astype(F32) * ya
              + g_ref[:, D_MODEL:2 * D_MODEL].astype(F32) * ym
              + g_ref[:, 2 * D_MODEL:3 * D_MODEL].astype(F32) * yc)
    o_ref[...] = x_ref[...] + _dot(merged.astype(BF16), wo_ref[...])


def _out_call(a, m, c, g, x2d, wa, wm, wc, wo, *, tm):
    n_tok = x2d.shape[0]
    row = lambda i: (i, 0)
    const = lambda i: (0, 0)
    resident = functools.partial(pl.BlockSpec, index_map=const, pipeline_mode=pl.Buffered(1))
    return pl.pallas_call(
        _out_kernel,
        grid=(n_tok // tm,),
        in_specs=[
            pl.BlockSpec((tm, D_A), row),
            pl.BlockSpec((tm, D_M), row),
            pl.BlockSpec((tm, D_C), row),
            pl.BlockSpec((tm, N_BRANCH * D_MODEL), row),
            pl.BlockSpec((tm, D_MODEL), row),
            resident((D_A, D_MODEL)),
            resident((D_M, D_MODEL)),
            resident((D_C, D_MODEL)),
            resident((D_MODEL, D_MODEL)),
        ],
        out_specs=pl.BlockSpec((tm, D_MODEL), row),
        out_shape=jax.ShapeDtypeStruct((n_tok, D_MODEL), F32),
        compiler_params=_cp(("parallel",)),
        name="outproj",
    )(a, m, c, g, x2d, wa, wm, wc, wo)


def _attn_kernel(q_ref, k_ref, vt0_ref, vt1_ref, km_ref, sza_ref, o_ref, bias_s, s_s, *, nb):
    vt_refs = (vt0_ref, vt1_ref)
    i = pl.program_id(2)
    own = (i * ATTN_Q) // MOBA_BLOCK
    ownf = own.astype(F32)
    q2 = q_ref[...]
    lane = lax.broadcasted_iota(jnp.int32, (ATTN_Q, LANES), 1)
    km_hi, km_mid, km_lo = _split3(km_ref[...])
    blk = lax.broadcasted_iota(jnp.int32, (nb, ATTN_Q), 0).astype(F32)

    qs = []
    for a in range(2):
        qa = jnp.where((lane // DH_A) == a, q2, jnp.zeros_like(q2))
        gs = _dot_nt(km_hi, qa) + _dot_nt(km_mid, qa) + _dot_nt(km_lo, qa)
        gs = jnp.where(blk < ownf, gs, -jnp.inf)
        bias = jnp.where(blk == ownf, 0.0, NEG)
        for _ in range(MOBA_TOPK):
            mx = jnp.max(gs, axis=0, keepdims=True)
            cand = jnp.where((gs == mx) & (mx > -jnp.inf), blk, float(nb))
            idx = jnp.min(cand, axis=0, keepdims=True)
            pick = blk == idx
            bias = jnp.where(pick, 0.0, bias)
            gs = jnp.where(pick, -jnp.inf, gs)
        for n in range(nb):
            bias_s[a, n] = jnp.broadcast_to(bias[n:n + 1, :], (8, ATTN_Q))
        qs.append(qa)

    def scores(j, a, kj):
        return _dot_nt(kj, qs[a]) + bias_s[a, j][0:1, :]

    def fold8(x, op):
        return op(x.reshape(MOBA_BLOCK // 8, 8, ATTN_Q), axis=0)

    def rows(j):
        return pl.ds(pl.multiple_of(j * MOBA_BLOCK, MOBA_BLOCK), MOBA_BLOCK)

    n_full = own // ATTN_GROUP

    def pass1(g, mx):
        mx = list(mx)
        for r in range(ATTN_GROUP):
            j = g * ATTN_GROUP + r
            kj = k_ref[rows(j), :]
            for a in range(2):
                s = scores(j, a, kj)
                s_s[a, rows(j), :] = s
                mx[a] = jnp.maximum(mx[a], fold8(s, jnp.max))
        return tuple(mx)

    neg8 = jnp.full((8, ATTN_Q), NEG, F32)
    mx = list(lax.fori_loop(0, n_full, pass1, (neg8, neg8)))
    kpos = lax.broadcasted_iota(jnp.int32, (MOBA_BLOCK, ATTN_Q), 0)
    qpos = lax.broadcasted_iota(jnp.int32, (MOBA_BLOCK, ATTN_Q), 1) + i * ATTN_Q
    for r in range(ATTN_GROUP):
        j = n_full * ATTN_GROUP + r
        kj = k_ref[rows(j), :]
        visible = (kpos + j * MOBA_BLOCK) <= qpos
        for a in range(2):
            s = jnp.where(visible, scores(j, a, kj), NEG)
            s_s[a, rows(j), :] = s
            mx[a] = jnp.maximum(mx[a], fold8(s, jnp.max))
    m = [jnp.max(mx[a], axis=0, keepdims=True) for a in range(2)]

    def accumulate(j, acc):
        return [acc[a] + _dot(vt_refs[a][j], jnp.exp2(s_s[a, rows(j), :] - m[a]).astype(BF16))
                for a in range(2)]

    def pass2(g, acc):
        for r in range(ATTN_GROUP):
            acc = accumulate(g * ATTN_GROUP + r, acc)
        return tuple(acc)

    zero = jnp.zeros((LANES, ATTN_Q), F32)
    acc = lax.fori_loop(0, n_full, pass2, (zero, zero))
    for r in range(ATTN_GROUP):
        acc = accumulate(n_full * ATTN_GROUP + r, acc)
    inv0 = 1.0 / acc[0][DH_A:DH_A + 1, :]
    inv1 = 1.0 / acc[1][0:1, :]
    drow = lax.broadcasted_iota(jnp.int32, (LANES, ATTN_Q), 0)
    ot = jnp.where((drow // DH_A) == 0, acc[0] * inv0, acc[1] * inv1)
    o_ref[...] = (ot.T * sza_ref[...].astype(F32)).astype(o_ref.dtype)


def _attn_call(q, kb, vb, kmean, sza, *, batch, seq):
    assert seq % (MOBA_BLOCK * ATTN_GROUP) == 0 and MOBA_BLOCK % ATTN_Q == 0
    nb = seq // MOBA_BLOCK
    nq = seq // ATTN_Q
    n_pair = D_A // LANES
    vt = vb.reshape(batch, nb, MOBA_BLOCK, D_A).transpose(0, 1, 3, 2)
    head_in_pair = (lax.broadcasted_iota(jnp.int32, (1, 1, D_A, 1), 2) // DH_A) % 2
    vts = [jnp.where(head_in_pair == a, vt, jnp.ones_like(vt)) for a in range(2)]
    vt_spec = pl.BlockSpec((None, nb, LANES, MOBA_BLOCK), lambda b, p, i: (b, 0, p, 0))
    return pl.pallas_call(
        functools.partial(_attn_kernel, nb=nb),
        grid=(batch, n_pair, nq),
        in_specs=[
            pl.BlockSpec((ATTN_Q, LANES), lambda b, p, i: (b * nq + i, p)),
            pl.BlockSpec((seq, LANES), lambda b, p, i: (b, p)),
            vt_spec, vt_spec,
            pl.BlockSpec((None, nb, LANES), lambda b, p, i: (b, 0, p)),
            pl.BlockSpec((ATTN_Q, LANES), lambda b, p, i: (b * nq + i, p)),
        ],
        out_specs=pl.BlockSpec((ATTN_Q, LANES), lambda b, p, i: (b * nq + i, p)),
        out_shape=jax.ShapeDtypeStruct((batch * seq, D_A), BF16),
        scratch_shapes=[pltpu.VMEM((2, nb, 8, ATTN_Q), F32), pltpu.VMEM((2, seq, ATTN_Q), F32)],
        compiler_params=_cp(("parallel", "parallel", "parallel")),
        name="moba_prompt",
    )(q, kb, vts[0], vts[1], kmean.reshape(batch, nb, D_A), sza)


def _mlstm_kernel(q_ref, kt_ref, k_ref, v_ref, gate_ref, gm_ref, mg_ref,
                  h_ref, c_out, n_out, m_out, c_s, n_s, m_s):
    c = pl.program_id(0)
    L = MLSTM_CHUNK

    @pl.when(c == 0)
    def _():
        c_s[...] = jnp.zeros_like(c_s)
        n_s[...] = jnp.zeros_like(n_s)
        m_s[...] = jnp.zeros_like(m_s)

    t_i = lax.broadcasted_iota(jnp.int32, (L, L), 0)
    s_i = lax.broadcasted_iota(jnp.int32, (L, L), 1)
    causal = s_i <= t_i

    def chain(b, h):
        sl = slice(h * DH_M, (h + 1) * DH_M)
        q = q_ref[b, :, sl]
        kt = kt_ref[b, sl, :]
        k = k_ref[b, :, sl]
        v = v_ref[b, :, sl]
        ig = gate_ref[b, h:h + 1, :]
        lf = gate_ref[b, H_M + h:H_M + h + 1, :]
        b_col = jnp.sum(jnp.where(causal, jnp.broadcast_to(lf, (L, L)), 0.0), axis=-1, keepdims=True)
        b_t = jnp.broadcast_to(b_col, (L, L))
        b_s = b_t.T
        m_prev = m_s[b, h]
        n_prev = n_s[b, h]
        c_prev = c_s[b, h]
        dmat = jnp.where(causal, b_t - b_s + jnp.broadcast_to(ig, (L, L)), NEG)
        inter = b_col + m_prev
        mt = jnp.maximum(inter, jnp.max(dmat, axis=-1, keepdims=True))
        w = jnp.exp(dmat - mt)
        dec = jnp.exp(inter - mt)
        s = _dot(q, kt) * w
        num = _dot(s.astype(BF16), v) + dec * _dot(q, c_prev.astype(BF16))
        den = jnp.sum(s, axis=-1, keepdims=True) + dec * jnp.sum(q.astype(F32) * n_prev, axis=-1, keepdims=True)
        hm = num / jnp.maximum(jnp.abs(den), jnp.exp(-mt))

        m_end = mt[L - 1:L, :]
        b_last = b_col[L - 1:L, :]
        b_row = b_s[0:1, :]
        dec_end = jnp.exp(b_last + m_prev - m_end)
        ws = jnp.exp(b_last - b_row + ig - m_end)
        kws = (kt.astype(F32) * ws).astype(BF16)
        c_s[b, h] = dec_end * c_prev + _dot(kws, v)
        ws_hi, ws_lo = _split2(jnp.broadcast_to(ws, (8, L)))
        n_s[b, h] = dec_end * n_prev + (_dot(ws_hi, k) + _dot(ws_lo, k))[0:1, :]
        m_s[b, h] = m_end

        hn = _layernorm_rows(hm) * mg_ref[:, sl] * gm_ref[b, :, sl].astype(F32)
        h_ref[b, :, sl] = hn.astype(h_ref.dtype)

    for b in range(q_ref.shape[0]):
        for h in range(H_M):
            chain(b, h)

    @pl.when(c == pl.num_programs(0) - 1)
    def _():
        c_out[...] = c_s[...]
        n_out[...] = n_s[...]
        m_out[...] = jnp.broadcast_to(m_s[...], m_out.shape)


def _mlstm_call(qm, km, vm, igf, gm, mg, *, batch, seq):
    L = MLSTM_CHUNK
    assert seq % L == 0
    nc = seq // L
    r3 = lambda t: t.reshape(batch, seq, D_M)
    kt = r3(km).transpose(0, 2, 1)
    gates = igf[:, :2 * H_M].reshape(batch, nc, L, 2 * H_M).transpose(1, 0, 3, 2)
    tile = pl.BlockSpec((batch, L, D_M), lambda c: (0, c, 0))
    state = lambda r, w: pl.BlockSpec((batch, H_M, r, w), lambda c: (0, 0, 0, 0))
    h3, c1, n1, m1 = pl.pallas_call(
        _mlstm_kernel,
        grid=(nc,),
        in_specs=[
            tile,
            pl.BlockSpec((batch, D_M, L), lambda c: (0, 0, c)),
            tile, tile,
            pl.BlockSpec((None, batch, 2 * H_M, L), lambda c: (c, 0, 0, 0)),
            tile,
            pl.BlockSpec((1, D_M), lambda c: (0, 0)),
        ],
        out_specs=[tile, state(DH_M, DH_M), state(1, DH_M), state(1, DH_M)],
        out_shape=[
            jax.ShapeDtypeStruct((batch, seq, D_M), BF16),
            jax.ShapeDtypeStruct((batch, H_M, DH_M, DH_M), F32),
            jax.ShapeDtypeStruct((batch, H_M, 1, DH_M), F32),
            jax.ShapeDtypeStruct((batch, H_M, 1, DH_M), F32),
        ],
        scratch_shapes=[pltpu.VMEM((batch, H_M, DH_M, DH_M), F32), pltpu.VMEM((batch, H_M, 1, DH_M), F32),
                        pltpu.VMEM((batch, H_M, 1, 1), F32)],
        compiler_params=_cp(("arbitrary",)),
        name="mlstm_prompt",
    )(r3(qm), kt, r3(km), r3(vm), gates, r3(gm), mg)
    return h3.reshape(batch * seq, D_M), c1, n1, m1


def _conv_epilogue(c, lg_ref, lb_ref, szc_ref, o_ref):
    y = _layernorm_rows(c) * lg_ref[...] + lb_ref[...]
    o_ref[...] = (_silu(y) * szc_ref[...].astype(F32)).astype(o_ref.dtype)


def _conv_kernel(u_ref, halo_ref, w_ref, cb_ref, lg_ref, lb_ref, szc_ref, o_ref, full):
    i = pl.program_id(1)
    tm = u_ref.shape[0]
    full[0:HALO, :] = jnp.where(i > 0, halo_ref[...], 0.0)
    full[HALO:HALO + tm, :] = u_ref[...]
    first = HALO - (CONV_W - 1)
    acc = jnp.zeros((tm, D_C), F32)
    for j in range(CONV_W):
        acc = acc + w_ref[j:j + 1, :] * full[first + j:first + j + tm, :]
    _conv_epilogue(acc + cb_ref[...], lg_ref, lb_ref, szc_ref, o_ref)


def _conv_call(u, wpad, cb, lg, lb, szc, *, batch, seq, tm):
    assert seq % tm == 0 and tm % HALO == 0
    nt = seq // tm
    row = lambda b, i: (b * nt + i, 0)
    const = lambda b, i: (0, 0)
    halo_idx = lambda b, i: (jnp.maximum((b * nt + i) * (tm // HALO) - 1, 0), 0)
    return pl.pallas_call(
        _conv_kernel,
        grid=(batch, nt),
        in_specs=[
            pl.BlockSpec((tm, D_C), row),
            pl.BlockSpec((HALO, D_C), halo_idx),
            pl.BlockSpec((HALO, D_C), const),
            pl.BlockSpec((1, D_C), const),
            pl.BlockSpec((1, D_C), const),
            pl.BlockSpec((1, D_C), const),
            pl.BlockSpec((tm, D_C), row),
        ],
        out_specs=pl.BlockSpec((tm, D_C), row),
        out_shape=jax.ShapeDtypeStruct((batch * seq, D_C), BF16),
        scratch_shapes=[pltpu.VMEM((HALO + tm, D_C), F32)],
        compiler_params=_cp(("parallel", "parallel")),
        name="conv_prompt",
    )(u, u, wpad, cb, lg, lb, szc)


def _attn_sample_kernel(pt_ref, q_ref, kn_ref, vn_ref, sza_ref, *refs, n_pages):
    del pt_ref
    k_refs = refs[:n_pages]
    v_refs = refs[n_pages:2 * n_pages]
    o_ref = refs[2 * n_pages]
    n_blk = n_pages * PAGE_SIZE // MOBA_BLOCK
    ppb = MOBA_BLOCK // PAGE_SIZE
    q = q_ref[0]
    lane = lax.broadcasted_iota(jnp.int32, (H_A, D_A), 1)
    sub = lax.broadcasted_iota(jnp.int32, (H_A, D_A), 0)
    hmask = (lane // DH_A) == sub
    qh = jnp.where(hmask, jnp.broadcast_to(q, (H_A, D_A)), 0.0)
    qhb = qh.astype(BF16)

    scores = []
    for pg in range(n_pages):
        kt = k_refs[pg][...].reshape(D_A, PAGE_SIZE)
        scores.append(_dot(qhb, kt.astype(BF16)))

    gs = []
    for j in range(n_blk):
        tot = jnp.sum(scores[j * ppb], axis=-1, keepdims=True)
        for r in range(1, ppb):
            tot = tot + jnp.sum(scores[j * ppb + r], axis=-1, keepdims=True)
        gs.append(tot * (1.0 / MOBA_BLOCK))

    picked = [jnp.zeros((H_A, 1), F32) for _ in range(n_blk)]
    for _ in range(min(MOBA_TOPK, n_blk)):
        mx = gs[0]
        for j in range(1, n_blk):
            mx = jnp.maximum(mx, gs[j])
        found = jnp.zeros((H_A, 1), F32)
        for j in range(n_blk):
            hit = jnp.where((gs[j] == mx) & (found == 0.0), 1.0, 0.0)
            picked[j] = jnp.maximum(picked[j], hit)
            found = jnp.maximum(found, hit)
            gs[j] = jnp.where(hit > 0.0, -jnp.inf, gs[j])

    s_own = jnp.sum(qh * kn_ref[0], axis=-1, keepdims=True)
    m = s_own
    for pg in range(n_pages):
        scores[pg] = scores[pg] + jnp.where(picked[pg // ppb] > 0.0, 0.0, NEG)
        m = jnp.maximum(m, jnp.max(scores[pg], axis=-1, keepdims=True))
    p_own = jnp.exp2(s_own - m)
    l = p_own
    o = p_own * vn_ref[0]
    for pg in range(n_pages):
        p = jnp.exp2(scores[pg] - m)
        l = l + jnp.sum(p, axis=-1, keepdims=True)
        vt = v_refs[pg][...].reshape(D_A, PAGE_SIZE)
        o = o + _dot_nt(p.astype(BF16), vt.astype(BF16))
    o = o / l
    orow = jnp.sum(jnp.where(hmask, o, 0.0), axis=0, keepdims=True)
    o_ref[0] = orow * sza_ref[0]


def _attn_sample_call(page_table, q, kn, vn, sza, cache_kt, cache_vt, layer):
    db, n_pages = page_table.shape
    assert MOBA_BLOCK % PAGE_SIZE == 0 and (n_pages * PAGE_SIZE) % MOBA_BLOCK == 0
    assert cache_kt.shape[2:] == (H_A, DH_A, PAGE_SIZE)
    rowspec = pl.BlockSpec((1, 1, D_A), lambda b, pt: (b, 0, 0))

    def page(pg):
        return pl.BlockSpec((None, None, H_A, DH_A, PAGE_SIZE), lambda b, pt: (pt[b, pg], layer, 0, 0, 0))

    grid_spec = pltpu.PrefetchScalarGridSpec(
        num_scalar_prefetch=1,
        grid=(db,),
        in_specs=[rowspec] * 4 + [page(pg) for pg in range(n_pages)] * 2,
        out_specs=rowspec,
    )
    r3 = lambda t: t.reshape(db, 1, D_A)
    return pl.pallas_call(
        functools.partial(_attn_sample_kernel, n_pages=n_pages),
        grid_spec=grid_spec,
        out_shape=jax.ShapeDtypeStruct((db, 1, D_A), F32),
        compiler_params=_cp(("parallel",)),
        name="moba_sample",
    )(page_table, r3(q), r3(kn), r3(vn), r3(sza), *([cache_kt] * n_pages), *([cache_vt] * n_pages)).reshape(db, D_A)


def _mlstm_sample_kernel(q_ref, k_ref, v_ref, if_ref, gm_ref, mg_ref, c_ref, n_ref, m_ref,
                         h_ref, c_out, n_out, m_out):
    bb = q_ref.shape[0]

    def body(bi, carry):
        for h in range(H_M):
            sl = slice(h * DH_M, (h + 1) * DH_M)
            q = q_ref[bi, :, sl]
            k = k_ref[bi, :, sl]
            v = v_ref[bi, :, sl]
            ig = if_ref[bi, :, h:h + 1]
            lf = if_ref[bi, :, H_M + h:H_M + h + 1]
            m0 = m_ref[bi, :, h:h + 1]
            c0 = c_ref[bi, h]
            n0 = n_ref[bi, h:h + 1, :]
            q_cols = jnp.broadcast_to(q, (DH_M, DH_M)).T
            k_cols = jnp.broadcast_to(k, (DH_M, DH_M)).T
            inter = lf + m0
            mt = jnp.maximum(inter, ig)
            w = jnp.exp(ig - mt)
            dec = jnp.exp(inter - mt)
            s = jnp.sum(q * k, axis=-1, keepdims=True) * w
            qc = jnp.sum(q_cols * c0, axis=0, keepdims=True)
            num = s * v + dec * qc
            den = s + dec * jnp.sum(q * n0, axis=-1, keepdims=True)
            hm = num / jnp.maximum(jnp.abs(den), jnp.exp(-mt))
            c_out[bi, h] = dec * c0 + w * (k_cols * v)
            n_out[bi, h:h + 1, :] = dec * n0 + w * k
            m_out[bi, :, h:h + 1] = mt
            h_ref[bi, :, sl] = _layernorm_rows(hm) * mg_ref[:, sl] * gm_ref[bi, :, sl]
        return carry

    lax.fori_loop(0, bb, body, 0)


def _mlstm_sample_call(qm, km, vm, igf, gm, mg, c0, n0, m0, layer, *, bb):
    db = qm.shape[0]
    assert db % bb == 0
    depth = m0.shape[0]
    row3 = lambda i: (i, 0, 0)
    r3 = lambda t: t.reshape(db, 1, t.shape[-1])
    tokspec = pl.BlockSpec((bb, 1, D_M), row3)
    h3, c1, n1, m1 = pl.pallas_call(
        _mlstm_sample_kernel,
        grid=(db // bb,),
        in_specs=[
            tokspec, tokspec, tokspec,
            pl.BlockSpec((bb, 1, IF_COLS), row3),
            tokspec,
            pl.BlockSpec((1, D_M), lambda i: (0, 0)),
            pl.BlockSpec((None, bb, H_M, DH_M, DH_M), lambda i: (layer, i, 0, 0, 0)),
            pl.BlockSpec((None, bb, H_M, DH_M), lambda i: (layer, i, 0, 0)),
            pl.BlockSpec((None, bb, 1, H_M), lambda i: (layer, i, 0, 0)),
        ],
        out_specs=[
            tokspec,
            pl.BlockSpec((bb, H_M, DH_M, DH_M), lambda i: (i, 0, 0, 0)),
            pl.BlockSpec((bb, H_M, DH_M), row3),
            pl.BlockSpec((bb, 1, H_M), row3),
        ],
        out_shape=[
            jax.ShapeDtypeStruct((db, 1, D_M), F32),
            jax.ShapeDtypeStruct((db, H_M, DH_M, DH_M), F32),
            jax.ShapeDtypeStruct((db, H_M, DH_M), F32),
            jax.ShapeDtypeStruct((db, 1, H_M), F32),
        ],
        compiler_params=_cp(("parallel",)),
        name="mlstm_sample",
    )(r3(qm), r3(km), r3(vm), r3(igf), r3(gm), mg, c0, n0, m0.reshape(depth, db, 1, H_M))
    return h3.reshape(db, D_M), c1, n1, m1.reshape(db, H_M)


def _conv_sample_kernel(st_ref, u_ref, w_ref, cb_ref, lg_ref, lb_ref, szc_ref, o_ref):
    acc = w_ref[CONV_W - 1:CONV_W, :] * u_ref[...]
    for j in range(CONV_W - 1):
        acc = acc + w_ref[j:j + 1, :] * st_ref[j]
    _conv_epilogue(acc + cb_ref[...], lg_ref, lb_ref, szc_ref, o_ref)


def _conv_sample_call(state_t, u, wpad, cb, lg, lb, szc, *, bb):
    db = u.shape[0]
    row = lambda i: (i, 0)
    const = lambda i: (0, 0)
    return pl.pallas_call(
        _conv_sample_kernel,
        grid=(db // bb,),
        in_specs=[
            pl.BlockSpec((CONV_W - 1, bb, D_C), lambda i: (0, i, 0)),
            pl.BlockSpec((bb, D_C), row),
            pl.BlockSpec((HALO, D_C), const),
            pl.BlockSpec((1, D_C), const),
            pl.BlockSpec((1, D_C), const),
            pl.BlockSpec((1, D_C), const),
            pl.BlockSpec((bb, D_C), row),
        ],
        out_specs=pl.BlockSpec((bb, D_C), row),
        out_shape=jax.ShapeDtypeStruct((db, D_C), F32),
        compiler_params=_cp(("parallel",)),
        name="conv_sample",
    )(state_t, u, wpad, cb, lg, lb, szc)


def _prep_weights(norm_g, w_in, b_igate, b_fgate, qn_g, kn_g, w_a_out, mh_norm_g, w_m_out,
                  conv_w, conv_b, cln_g, cln_b, w_c_out, w_out):
    depth = w_in.shape[0]
    gate0 = 4 * D_A + 5 * D_M
    w1 = w_in[:, :, :gate0].astype(BF16)
    w2 = w_in[:, :, gate0 + 2 * H_M:].astype(BF16)
    assert w1.shape[-1] == N_SEG1 * SEG and w2.shape[-1] == (N_SEG - N_SEG1) * SEG
    wif = jnp.concatenate([w_in[:, :, gate0:gate0 + 2 * H_M],
                           jnp.zeros((depth, D_MODEL, IF_COLS - 2 * H_M), w_in.dtype)], axis=-1).astype(BF16)
    ifb = jnp.concatenate([b_igate, b_fgate, jnp.zeros((depth, IF_COLS - 2 * H_M), F32)], axis=-1)
    gbd = jnp.asarray(np.kron(np.eye(H_A), np.ones((DH_A, DH_A))), BF16)
    wpad = jnp.concatenate([conv_w, jnp.zeros((depth, HALO - CONV_W, D_C), conv_w.dtype)], axis=1)
    layers = []
    for l in range(depth):
        layers.append(dict(
            ng=norm_g[l][None], w1=w1, w2=w2, wif=wif, layer=l, gbd=gbd, ifb=ifb[l][None],
            qg=jnp.tile(qn_g[l], H_A)[None], kg=jnp.tile(kn_g[l], H_A)[None],
            wa=w_a_out[l].astype(BF16), wm=w_m_out[l].astype(BF16), wc=w_c_out[l].astype(BF16),
            wo=w_out[l].astype(BF16), mg=mh_norm_g[l][None], wpad=wpad[l],
            cb=conv_b[l][None], lg=cln_g[l][None], lb=cln_b[l][None]))
    return layers


def _prompt_layer(x2d, p, *, batch, seq):
    (q, kf, kb, vf, vb, sza, qm, km, vm, gm, igf, u, szc, g, kmean) = _proj_call(
        x2d, p["ng"], p["w1"], p["w2"], p["wif"], p["gbd"], p["qg"], p["kg"], p["ifb"], p["layer"],
        tm=MOBA_BLOCK, act_dtype=BF16, with_kmean=True)
    a_in = _attn_call(q, kb, vb, kmean, sza, batch=batch, seq=seq)
    m_in, c1, n1, m1 = _mlstm_call(qm, km, vm, igf, gm, p["mg"], batch=batch, seq=seq)
    c_in = _conv_call(u, p["wpad"], p["cb"], p["lg"], p["lb"], szc, batch=batch, seq=seq, tm=MOBA_BLOCK)
    y = _out_call(a_in, m_in, c_in, g, x2d, p["wa"], p["wm"], p["wc"], p["wo"], tm=MOBA_BLOCK)
    conv_state = u.reshape(batch, seq, D_C)[:, seq - (CONV_W - 1):]
    return y, kf, vf, c1, n1[:, :, 0], m1[:, :, 0, 0], conv_state


def _sample_layer(x2d, p, page_table, cache_k, cache_v, st_c, st_n, st_m, st_conv, layer):
    db = x2d.shape[0]
    (q, kf, _, vf, _, sza, qm, km, vm, gm, igf, u, szc, g) = _proj_call(
        x2d, p["ng"], p["w1"], p["w2"], p["wif"], p["gbd"], p["qg"], p["kg"], p["ifb"], p["layer"],
        tm=db, act_dtype=F32, with_kmean=False)
    a_in = _attn_sample_call(page_table, q, kf, vf, sza, cache_k, cache_v, layer)
    m_in, c1, n1, m1 = _mlstm_sample_call(qm, km, vm, igf, gm, p["mg"], st_c, st_n, st_m, layer, bb=8)
    conv_hist = st_conv[layer]
    c_in = _conv_sample_call(conv_hist.transpose(1, 0, 2), u, p["wpad"], p["cb"], p["lg"], p["lb"], szc, bb=8)
    y = _out_call(a_in, m_in, c_in, g, x2d, p["wa"], p["wm"], p["wc"], p["wo"], tm=db)
    conv_state = jnp.concatenate([conv_hist[:, 1:], u[:, None, :]], axis=1)
    return y, kf, vf, c1, n1, m1, conv_state


def kernel(x_prompt, x_sample, cache_k, cache_v, state_mlstm_C, state_mlstm_n, state_mlstm_m, state_conv,
           page_table, norm_g, w_in, b_igate, b_fgate, qn_g, kn_g, w_a_out, mh_norm_g, w_m_out,
           conv_w, conv_b, cln_g, cln_b, w_c_out, w_out):
    batch, seq, _ = x_prompt.shape
    db, dseq, _ = x_sample.shape
    assert dseq == 1
    depth = w_in.shape[0]
    layers = _prep_weights(norm_g, w_in, b_igate, b_fgate, qn_g, kn_g, w_a_out, mh_norm_g, w_m_out,
                           conv_w, conv_b, cln_g, cln_b, w_c_out, w_out)
    cache_kt = cache_k.transpose(0, 1, 3, 4, 2)
    cache_vt = cache_v.transpose(0, 1, 3, 4, 2)
    yp = x_prompt.reshape(batch * seq, D_MODEL)
    ys = x_sample.reshape(db, D_MODEL)
    pr, sa = [], []
    for l in range(depth):
        yp, *rest = _prompt_layer(yp, layers[l], batch=batch, seq=seq)
        pr.append(rest)
        ys, *rest = _sample_layer(ys, layers[l], page_table, cache_kt, cache_vt,
                                  state_mlstm_C, state_mlstm_n, state_mlstm_m, state_conv, l)
        sa.append(rest)

    def stack(rows, idx, axis):
        return jnp.stack([r[idx] for r in rows], axis=axis)

    k_prompt = jnp.stack([r[0].reshape(batch, seq, H_A, DH_A) for r in pr], axis=1)
    v_prompt = jnp.stack([r[1].reshape(batch, seq, H_A, DH_A) for r in pr], axis=1)
    k_sample = stack(sa, 0, 1).reshape(db, depth, 1, H_A, DH_A)
    v_sample = stack(sa, 1, 1).reshape(db, depth, 1, H_A, DH_A)
    return (yp.reshape(batch, seq, D_MODEL), ys.reshape(db, 1, D_MODEL),
            k_prompt, v_prompt, stack(pr, 2, 0), stack(pr, 3, 0), stack(pr, 4, 0), stack(pr, 5, 0),
            k_sample, v_sample, stack(sa, 2, 0), stack(sa, 3, 0), stack(sa, 4, 0), stack(sa, 5, 0))
```
